```python
import jax, jax.numpy as jnp
from jax import lax
import numpy as np

D_MODEL = 1024
BATCH = 8
SEQ = 4096
DEPTH = 4

D_DELTA = D_MODEL // 2
D_SGU = D_MODEL - D_DELTA
HEAD_DIM = 128
N_HEADS_DELTA = D_DELTA // HEAD_DIM
CONV_WIDTH = 5
DELTA_CHUNK = 64
SGU_CHUNK = 128
SGU_GROUP_DIM = 128
N_HEADS_SGU = D_SGU // SGU_GROUP_DIM
N_EXPERTS = 16
N_EXPERT_GROUPS = 4
EXPERTS_PER_GROUP = N_EXPERTS // N_EXPERT_GROUPS
TOP_K = 2
D_FF_EXPERT = D_MODEL
DEEPNORM_ALPHA = (2.0 * DEPTH) ** 0.25
DEEPNORM_BETA = (8.0 * DEPTH) ** -0.25
LN_EPS = 1e-5
RMS_EPS = 1e-6
L2_EPS = 1e-6
PROJ_SPLITS = (D_DELTA, D_DELTA, D_DELTA, D_DELTA,
               N_HEADS_DELTA, N_HEADS_DELTA, N_HEADS_DELTA, N_HEADS_DELTA,
               D_SGU, D_SGU)
PROJ_COLS = 4 * D_DELTA + 4 * N_HEADS_DELTA + 2 * D_SGU

kernel_name = "hybrid_delta_sgu_moe_deepnorm"


def layer_norm(x, g, b):
    xf = x.astype(jnp.float32)
    mu = jnp.mean(xf, -1, keepdims=True)
    var = jnp.mean(jnp.square(xf - mu), -1, keepdims=True)
    return ((xf - mu) * lax.rsqrt(var + LN_EPS) * g + b).astype(x.dtype)


def l2_normalize(t):
    return t * lax.rsqrt(jnp.sum(jnp.square(t), -1, keepdims=True) + L2_EPS)


def centred_depthwise_conv(x, w):
    pad = CONV_WIDTH // 2
    return lax.conv_general_dilated(
        x, w[:, None, :], window_strides=(1,), padding=[(pad, pad)],
        dimension_numbers=("NWC", "WIO", "NWC"), feature_group_count=x.shape[-1])


def gated_delta_rule_chunked(q, k, v, g, beta):
    b, s, h, dk = q.shape
    dv = v.shape[-1]
    n, c = s // DELTA_CHUNK, DELTA_CHUNK

    def chunk(t):
        t = t.reshape((b, n, c, h) + t.shape[3:])
        return jnp.moveaxis(t, (1, 3), (0, 2))

    qc = chunk(q) * (dk ** -0.5)
    kc, vc, bc = chunk(k), chunk(v), chunk(beta)
    gc = jnp.cumsum(chunk(g), axis=-1)
    idx = jnp.arange(c)
    lower_incl = idx[:, None] >= idx[None, :]
    strict = idx[:, None] > idx[None, :]
    diff = gc[..., :, None] - gc[..., None, :]
    decay = jnp.where(lower_incl, jnp.exp(jnp.where(lower_incl, diff, 0.0)), 0.0)
    kb = kc * bc[..., None]
    a_strict = jnp.where(strict, jnp.einsum("nbhid,nbhjd->nbhij", kb, kc) * decay, 0.0)
    eye = jnp.eye(c, dtype=q.dtype)
    t_mat = lax.linalg.triangular_solve(
        a_strict + eye, jnp.broadcast_to(eye, a_strict.shape),
        left_side=True, lower=True, unit_diagonal=True)
    u = jnp.einsum("nbhij,nbhjd->nbhid", t_mat, vc * bc[..., None])
    w = jnp.einsum("nbhij,nbhjd->nbhid", t_mat, kb * jnp.exp(gc)[..., None])
    qk = jnp.where(lower_incl, jnp.einsum("nbhid,nbhjd->nbhij", qc, kc) * decay, 0.0)

    def step(state, xs):
        q_i, k_i, u_i, w_i, g_i, qk_i = xs
        v_new = u_i - jnp.einsum("bhck,bhkv->bhcv", w_i, state)
        o_i = (jnp.einsum("bhck,bhkv->bhcv", q_i * jnp.exp(g_i)[..., None], state)
               + jnp.einsum("bhij,bhjv->bhiv", qk_i, v_new))
        g_last = g_i[..., -1:]
        k_dec = k_i * jnp.exp(g_last - g_i)[..., None]
        state = state * jnp.exp(g_last)[..., None] + jnp.einsum("bhck,bhcv->bhkv", k_dec, v_new)
        return state, o_i

    state0 = jnp.zeros((b, h, dk, dv), q.dtype)
    _, o = lax.scan(step, state0, (qc, kc, u, w, gc, qk))
    return jnp.moveaxis(o, (0, 2), (1, 3)).reshape(b, s, h, dv)


def delta_mixer(q, k, v, z, a_f, a_b, b_f, b_b, conv_w, a_log_f, dt_bias_f,
                a_log_b, dt_bias_b, norm_w):
    bsz, s, _ = q.shape
    qkv = jax.nn.silu(centred_depthwise_conv(jnp.concatenate([q, k, v], -1), conv_w))
    q, k, v = jnp.split(qkv.astype(jnp.float32), 3, axis=-1)
    heads = lambda t: t.reshape(bsz, s, N_HEADS_DELTA, HEAD_DIM)
    q, k, v = l2_normalize(heads(q)), l2_normalize(heads(k)), heads(v)
    g_f = -jnp.exp(a_log_f.astype(jnp.float32)) * jax.nn.softplus(a_f.astype(jnp.float32) + dt_bias_f)
    g_b = -jnp.exp(a_log_b.astype(jnp.float32)) * jax.nn.softplus(a_b.astype(jnp.float32) + dt_bias_b)
    beta_f = jax.nn.sigmoid(b_f.astype(jnp.float32))
    beta_b = jax.nn.sigmoid(b_b.astype(jnp.float32))
    flip = lambda t: jnp.flip(t, axis=1)
    o_f = gated_delta_rule_chunked(q, k, v, g_f, beta_f)
    o_b = flip(gated_delta_rule_chunked(flip(q), flip(k), flip(v), flip(g_b), flip(beta_b)))
    o = o_f + o_b
    o = o * lax.rsqrt(jnp.mean(jnp.square(o), -1, keepdims=True) + RMS_EPS) * norm_w
    o = o * jax.nn.silu(heads(z).astype(jnp.float32))
    return o.reshape(bsz, s, D_DELTA).astype(z.dtype)


def sgu_mixer(u, v, ln_g, ln_b, w_spatial, b_spatial):
    bsz, s, _ = u.shape
    u = jax.nn.gelu(u, approximate=False)
    v = layer_norm(jax.nn.gelu(v, approximate=False), ln_g, ln_b)
    n = s // SGU_CHUNK
    vc = v.reshape(bsz, n, SGU_CHUNK, N_HEADS_SGU, SGU_GROUP_DIM)
    mixed = jnp.einsum("gpq,bnqgd->bnpgd", w_spatial, vc) + b_spatial.T[:, :, None]
    return u * mixed.reshape(bsz, s, D_SGU)


def moe(x, w_router, router_bias, w_gate, w_up, w_down):
    bsz, s, d = x.shape
    xt = x.reshape(-1, d)
    n_tok = xt.shape[0]
    scores = jax.nn.sigmoid(jnp.dot(xt.astype(jnp.float32), w_router.astype(jnp.float32)))
    sel = (scores + router_bias).reshape(n_tok, N_EXPERT_GROUPS, EXPERTS_PER_GROUP)
    group_score = jnp.sum(lax.top_k(sel, TOP_K)[0], -1)
    best = jnp.argmax(group_score, -1)
    in_group = jnp.arange(N_EXPERT_GROUPS)[None, :] == best[:, None]
    sel = jnp.where(in_group[..., None], sel, -jnp.inf).reshape(n_tok, N_EXPERTS)
    _, top_idx = lax.top_k(sel, TOP_K)
    gates = jnp.take_along_axis(scores, top_idx, -1)
    gates = gates / jnp.sum(gates, -1, keepdims=True)
    expert_ids = top_idx.reshape(-1)
    order = jnp.argsort(expert_ids)
    tok = order // TOP_K
    xs = xt[tok]
    group_sizes = jnp.bincount(expert_ids, length=N_EXPERTS).astype(jnp.int32)
    hdn = jax.nn.silu(lax.ragged_dot(xs, w_gate, group_sizes)) * lax.ragged_dot(xs, w_up, group_sizes)
    y = lax.ragged_dot(hdn, w_down, group_sizes) * gates.reshape(-1)[order][:, None].astype(x.dtype)
    out = jax.ops.segment_sum(y, tok, num_segments=n_tok)
    return out.reshape(bsz, s, d)


def setup_inputs(seed: int = 0) -> dict:
    key = jax.random.key(seed)
    ks = jax.random.split(key, 24)
    f32 = jnp.float32
    nrm = lambda k, shape, scale: jax.random.normal(k, shape, f32) * scale

    def dt_bias(k):
        dt = jnp.exp(jax.random.uniform(k, (DEPTH, N_HEADS_DELTA), f32, np.log(1e-3), np.log(1e-1)))
        return dt + jnp.log(-jnp.expm1(-dt))

    def a_log(k):
        return jnp.log(jax.random.uniform(k, (DEPTH, N_HEADS_DELTA), f32, 1.0, 16.0))

    return {
        "x": nrm(ks[0], (BATCH, SEQ, D_MODEL), 1.0),
        "w_in": nrm(ks[1], (DEPTH, D_MODEL, PROJ_COLS), D_MODEL ** -0.5),
        "conv_w": nrm(ks[2], (DEPTH, CONV_WIDTH, 3 * D_DELTA), CONV_WIDTH ** -0.5),
        "a_log_f": a_log(ks[3]),
        "dt_bias_f": dt_bias(ks[4]),
        "a_log_b": a_log(ks[5]),
        "dt_bias_b": dt_bias(ks[6]),
        "delta_norm_w": 1.0 + nrm(ks[7], (DEPTH, HEAD_DIM), 0.02),
        "sgu_ln_g": 1.0 + nrm(ks[8], (DEPTH, D_SGU), 0.02),
        "sgu_ln_b": nrm(ks[9], (DEPTH, D_SGU), 0.02),
        "w_spatial": nrm(ks[10], (DEPTH, N_HEADS_SGU, SGU_CHUNK, SGU_CHUNK), SGU_CHUNK ** -0.5),
        "b_spatial": 1.0 + nrm(ks[11], (DEPTH, N_HEADS_SGU, SGU_CHUNK), 0.02),
        "w_out": nrm(ks[12], (DEPTH, D_MODEL, D_MODEL), DEEPNORM_BETA * D_MODEL ** -0.5),
        "ln1_g": 1.0 + nrm(ks[13], (DEPTH, D_MODEL), 0.02),
        "ln1_b": nrm(ks[14], (DEPTH, D_MODEL), 0.02),
        "w_router": nrm(ks[15], (D_MODEL, N_EXPERTS), D_MODEL ** -0.5),
        "router_bias": nrm(ks[16], (N_EXPERTS,), 0.01),
        "w_gate": nrm(ks[17], (DEPTH, N_EXPERTS, D_MODEL, D_FF_EXPERT), D_MODEL ** -0.5),
        "w_up": nrm(ks[18], (DEPTH, N_EXPERTS, D_MODEL, D_FF_EXPERT), D_MODEL ** -0.5),
        "w_down": nrm(ks[19], (DEPTH, N_EXPERTS, D_FF_EXPERT, D_MODEL), DEEPNORM_BETA * D_FF_EXPERT ** -0.5),
        "ln2_g": 1.0 + nrm(ks[20], (DEPTH, D_MODEL), 0.02),
        "ln2_b": nrm(ks[21], (DEPTH, D_MODEL), 0.02),
    }


def reference(x, w_in, conv_w, a_log_f, dt_bias_f, a_log_b, dt_bias_b, delta_norm_w,
              sgu_ln_g, sgu_ln_b, w_spatial, b_spatial, w_out, ln1_g, ln1_b,
              w_router, router_bias, w_gate, w_up, w_down, ln2_g, ln2_b):
    offsets = np.cumsum(PROJ_SPLITS)[:-1].tolist()
    for l in range(DEPTH):
        proj = jnp.einsum("bsd,dp->bsp", x, w_in[l])
        q, k, v, z, a_f, a_b, b_f, b_b, u_s, v_s = jnp.split(proj, offsets, axis=-1)
        o_delta = delta_mixer(q, k, v, z, a_f, a_b, b_f, b_b, conv_w[l], a_log_f[l], dt_bias_f[l],
                              a_log_b[l], dt_bias_b[l], delta_norm_w[l])
        o_sgu = sgu_mixer(u_s, v_s, sgu_ln_g[l], sgu_ln_b[l], w_spatial[l], b_spatial[l])
        mix = jnp.einsum("bsc,cd->bsd", jnp.concatenate([o_delta, o_sgu], -1), w_out[l])
        x = layer_norm(DEEPNORM_ALPHA * x + mix, ln1_g[l], ln1_b[l])
        ffn = moe(x, w_router, router_bias, w_gate[l], w_up[l], w_down[l])
        x = layer_norm(DEEPNORM_ALPHA * x + ffn, ln2_g[l], ln2_b[l])
    return x
```

```python
import functools

import jax
import jax.numpy as jnp
from jax import lax
from jax.experimental import pallas as pl
from jax.experimental.pallas import tpu as pltpu

F32 = jnp.float32
BF16 = jnp.bfloat16

D_MODEL = 1024
D_DELTA = 512
D_SGU = 512
HEAD_DIM = 128
N_HEADS = 4
CONV_WIDTH = 5
CHUNK = 64
SGU_CHUNK = 128
N_GROUPS_SGU = 4
N_EXPERTS = 16
N_EXPERT_GROUPS = 4
EXPERTS_PER_GROUP = 4
LN_EPS = 1e-5
RMS_EPS = 1e-6
L2_EPS = 1e-6

MAIN_COLS = 6 * 512
GATE_COLS = 128
HALO_ROWS = 16
VMEM_LIMIT = 56 * 1024 * 1024


def _bdot(a, b):
    return jnp.dot(a.astype(BF16), b.astype(BF16), preferred_element_type=F32)


def _bdot_nt(a, b):
    return lax.dot_general(a.astype(BF16), b.astype(BF16), (((1,), (1,)), ((), ())),
                           preferred_element_type=F32)


def _bdot_tn(a, b):
    return lax.dot_general(a.astype(BF16), b.astype(BF16), (((0,), (0,)), ((), ())),
                           preferred_element_type=F32)


def _layer_norm(y, g, b):
    mu = jnp.mean(y, -1, keepdims=True)
    d = y - mu
    var = jnp.mean(d * d, -1, keepdims=True)
    return d * lax.rsqrt(var + LN_EPS) * g + b


def _proj_kernel(x_ref, w_ref, wg_ref, h_ref, gl_ref, *, col_chunk):
    xb = x_ref[...].astype(BF16)
    for c in range(MAIN_COLS // col_chunk):
        sl = slice(c * col_chunk, (c + 1) * col_chunk)
        h_ref[:, sl] = jnp.dot(xb, w_ref[:, sl], preferred_element_type=F32).astype(BF16)
    gl_ref[...] = jnp.dot(xb, wg_ref[...], preferred_element_type=F32)


def _proj(x2d, w_main, w_gate, tm):
    n = x2d.shape[0]
    return pl.pallas_call(
        functools.partial(_proj_kernel, col_chunk=1024),
        grid=(n // tm,),
        in_specs=[
            pl.BlockSpec((tm, D_MODEL), lambda i: (i, 0)),
            pl.BlockSpec((D_MODEL, MAIN_COLS), lambda i: (0, 0)),
            pl.BlockSpec((D_MODEL, GATE_COLS), lambda i: (0, 0)),
        ],
        out_specs=[
            pl.BlockSpec((tm, MAIN_COLS), lambda i: (i, 0)),
            pl.BlockSpec((tm, GATE_COLS), lambda i: (i, 0)),
        ],
        out_shape=[
            jax.ShapeDtypeStruct((n, MAIN_COLS), BF16),
            jax.ShapeDtypeStruct((n, GATE_COLS), F32),
        ],
        compiler_params=pltpu.CompilerParams(
            dimension_semantics=("parallel",), vmem_limit_bytes=VMEM_LIMIT),
        name="proj",
    )(x2d, w_main, w_gate)


def _conv_taps(ext, w, lo, n):
    rows = ext.shape[0]
    acc = None
    for j in range(CONV_WIDTH):
        shift = (-(j - CONV_WIDTH // 2)) % rows
        r = ext if shift == 0 else pltpu.roll(ext, shift, 0)
        term = r[lo:lo + n] * w[j:j + 1]
        acc = term if acc is None else acc + term
    return acc


def _conv_post(y):
    s = y * jax.nn.sigmoid(y)
    outs = []
    for hh in range(3 * N_HEADS):
        seg = s[:, hh * HEAD_DIM:(hh + 1) * HEAD_DIM]
        if hh < 2 * N_HEADS:
            ss = jnp.sum(seg * seg, -1, keepdims=True)
            seg = seg * lax.rsqrt(ss + L2_EPS)
            if hh < N_HEADS:
                seg = seg * (HEAD_DIM ** -0.5)
        outs.append(seg)
    return jnp.concatenate(outs, -1).astype(BF16)


def _conv_kernel(cur_ref, prev_ref, next_ref, gl_ref, cw_ref, gp_ref, qkv_ref, gcol_ref,
                 *, tiles_per_seq):
    i = pl.program_id(0)
    tb = cur_ref.shape[0]
    pos = i % tiles_per_seq
    cur = cur_ref[...].astype(F32)
    prev = jnp.where(pos == 0, 0.0, prev_ref[...].astype(F32))
    nxt = jnp.where(pos == tiles_per_seq - 1, 0.0, next_ref[...].astype(F32))
    w = cw_ref[...]
    h = HALO_ROWS
    qkv_ref[...] = _conv_post(_conv_taps(cur, w, 0, tb))
    top = _conv_taps(jnp.concatenate([prev, cur[:2 * h]], 0), w, h, h)
    qkv_ref[0:h, :] = _conv_post(top)
    bot = _conv_taps(jnp.concatenate([cur[tb - 2 * h:], nxt], 0), w, h, h)
    qkv_ref[tb - h:tb, :] = _conv_post(bot)

    gl = gl_ref[...]
    neg_a = -jnp.exp(gp_ref[0:1, :])
    xg = gl + gp_ref[1:2, :]
    softplus = jnp.maximum(xg, 0.0) + jnp.log1p(jnp.exp(-jnp.abs(xg)))
    g = neg_a * softplus
    beta = jax.nn.sigmoid(gl)
    lane = lax.broadcasted_iota(jnp.int32, (CHUNK, GATE_COLS), 1)
    r = lax.broadcasted_iota(jnp.int32, (2 * CHUNK, CHUNK), 0)
    c = lax.broadcasted_iota(jnp.int32, (2 * CHUNK, CHUNK), 1)
    tri = jnp.where(r < CHUNK, jnp.where(r >= c, 1.0, 0.0), jnp.where(r - CHUNK <= c, 1.0, 0.0))
    for ch in range(tb // CHUNK):
        sl = slice(ch * CHUNK, (ch + 1) * CHUNK)
        cs = jnp.dot(tri, g[sl], precision=lax.Precision.HIGHEST, preferred_element_type=F32)
        out = jnp.where(lane < N_HEADS, cs[:CHUNK],
                        jnp.where(lane < 2 * N_HEADS, cs[CHUNK:], beta[sl]))
        gcol_ref[sl, :] = out[:, :4 * N_HEADS]


def _conv(h_main, gl, conv_w8, gate_params, seq, tb):
    n = h_main.shape[0]
    hb = tb // HALO_ROWS
    last_hblk = n // HALO_ROWS - 1
    return pl.pallas_call(
        functools.partial(_conv_kernel, tiles_per_seq=seq // tb),
        grid=(n // tb,),
        in_specs=[
            pl.BlockSpec((tb, 3 * D_DELTA), lambda i: (i, 0)),
            pl.BlockSpec((HALO_ROWS, 3 * D_DELTA), lambda i: (jnp.maximum(i * hb - 1, 0), 0)),
            pl.BlockSpec((HALO_ROWS, 3 * D_DELTA),
                         lambda i: (jnp.minimum((i + 1) * hb, last_hblk), 0)),
            pl.BlockSpec((tb, GATE_COLS), lambda i: (i, 0)),
            pl.BlockSpec((8, 3 * D_DELTA), lambda i: (0, 0)),
            pl.BlockSpec((8, GATE_COLS), lambda i: (0, 0)),
        ],
        out_specs=[
            pl.BlockSpec((tb, 3 * D_DELTA), lambda i: (i, 0)),
            pl.BlockSpec((tb, 4 * N_HEADS), lambda i: (i, 0)),
        ],
        out_shape=[
            jax.ShapeDtypeStruct((n, 3 * D_DELTA), BF16),
            jax.ShapeDtypeStruct((n, 4 * N_HEADS), F32),
        ],
        compiler_params=pltpu.CompilerParams(
            dimension_semantics=("parallel",), vmem_limit_bytes=VMEM_LIMIT),
        name="conv",
    )(h_main, h_main, h_main, gl, conv_w8, gate_params)


def _delta_direction(qkv, gcol, grow, s_ref, o_ref, fwd):
    c, nh = CHUNK, N_HEADS
    hc = nh * c
    off = 0 if fwd else nh

    def stack(base):
        return jnp.concatenate(
            [qkv[:, (base + h) * HEAD_DIM:(base + h + 1) * HEAD_DIM] for h in range(nh)], 0
        ).astype(F32)

    q, k, v = stack(0), stack(nh), stack(2 * nh)
    gc_col = jnp.concatenate([gcol[:, off + h:off + h + 1] for h in range(nh)], 0)
    beta = jnp.concatenate([gcol[:, 2 * nh + off + h:2 * nh + off + h + 1] for h in range(nh)], 0)
    gc_row = jnp.concatenate([grow[off + h:off + h + 1, :] for h in range(nh)], 1)
    edge = c - 1 if fwd else 0
    g_last = [gcol[edge:edge + 1, off + h:off + h + 1] for h in range(nh)]
    g_last_col = jnp.concatenate([jnp.broadcast_to(g, (c, 1)) for g in g_last], 0)

    ri = lax.broadcasted_iota(jnp.int32, (hc, hc), 0)
    ci = lax.broadcasted_iota(jnp.int32, (hc, hc), 1)
    same64 = (ri >> 6) == (ci >> 6)
    same32 = (ri >> 5) == (ci >> 5)
    same16 = (ri >> 4) == (ci >> 4)
    incl = same64 & ((ri >= ci) if fwd else (ri <= ci))
    strict = incl & (ri != ci)

    decay = jnp.where(incl, jnp.exp(jnp.where(incl, gc_col - gc_row, 0.0)), 0.0)
    kb = k * beta
    a = jnp.where(strict, _bdot_nt(kb, k) * decay, 0.0)

    eye = jnp.where(ri == ci, 1.0, 0.0)
    p = jnp.where(same16, a, 0.0)
    t = eye - p
    for _ in range(3):
        p = _bdot(p, p)
        t = t + _bdot(t, p)
    a32 = jnp.where(same32, jnp.where(same16, 0.0, a), 0.0)
    t = t - _bdot(t, _bdot(a32, t))
    a64 = jnp.where(same32, 0.0, a)
    t = t - _bdot(t, _bdot(a64, t))

    uw = _bdot(t, jnp.concatenate([v * beta, kb * jnp.exp(gc_col)], 1))
    u, w = uw[:, :HEAD_DIM], uw[:, HEAD_DIM:]
    qk = jnp.where(incl, _bdot_nt(q, k) * decay, 0.0)
    qg = q * jnp.exp(gc_col)
    k_dec = k * jnp.exp(g_last_col - gc_col)

    v_new, o_inter = [], []
    for h in range(nh):
        rows = slice(h * c, (h + 1) * c)
        ws = _bdot(jnp.concatenate([w[rows], qg[rows]], 0), s_ref[h])
        v_new.append(u[rows] - ws[:c])
        o_inter.append(ws[c:])
    v_new_all = jnp.concatenate(v_new, 0)
    o = jnp.concatenate(o_inter, 0) + _bdot(qk, v_new_all)
    for h in range(nh):
        rows = slice(h * c, (h + 1) * c)
        s_ref[h] = s_ref[h] * jnp.exp(g_last[h]) + _bdot_tn(k_dec[rows], v_new[h])
        o_ref[:, h * HEAD_DIM:(h + 1) * HEAD_DIM] = o[rows]


def _delta_kernel(qkv_f_ref, gcol_f_ref, grow_f_ref, qkv_b_ref, gcol_b_ref, grow_b_ref,
                  of_ref, ob_ref, sf_ref, sb_ref):
    @pl.when(pl.program_id(1) == 0)
    def _():
        sf_ref[...] = jnp.zeros_like(sf_ref)
        sb_ref[...] = jnp.zeros_like(sb_ref)

    _delta_direction(qkv_f_ref[...], gcol_f_ref[...], grow_f_ref[0], sf_ref, of_ref, True)
    _delta_direction(qkv_b_ref[...], gcol_b_ref[...], grow_b_ref[0], sb_ref, ob_ref, False)


def _delta(qkv, gcol, grow, batch, seq):
    n = qkv.shape[0]
    nc = seq // CHUNK
    fwd = lambda b, j: (b * nc + j, 0)
    bwd = lambda b, j: (b * nc + nc - 1 - j, 0)
    fwd3 = lambda b, j: (b * nc + j, 0, 0)
    bwd3 = lambda b, j: (b * nc + nc - 1 - j, 0, 0)
    return pl.pallas_call(
        _delta_kernel,
        grid=(batch, nc),
        in_specs=[
            pl.BlockSpec((CHUNK, 3 * D_DELTA), fwd),
            pl.BlockSpec((CHUNK, 4 * N_HEADS), fwd),
            pl.BlockSpec((1, 4 * N_HEADS, CHUNK), fwd3),
            pl.BlockSpec((CHUNK, 3 * D_DELTA), bwd),
            pl.BlockSpec((CHUNK, 4 * N_HEADS), bwd),
            pl.BlockSpec((1, 4 * N_HEADS, CHUNK), bwd3),
        ],
        out_specs=[
            pl.BlockSpec((CHUNK, D_DELTA), fwd),
            pl.BlockSpec((CHUNK, D_DELTA), bwd),
        ],
        out_shape=[
            jax.ShapeDtypeStruct((n, D_DELTA), F32),
            jax.ShapeDtypeStruct((n, D_DELTA), F32),
        ],
        scratch_shapes=[
            pltpu.VMEM((N_HEADS, HEAD_DIM, HEAD_DIM), F32),
            pltpu.VMEM((N_HEADS, HEAD_DIM, HEAD_DIM), F32),
        ],
        compiler_params=pltpu.CompilerParams(
            dimension_semantics=("arbitrary", "arbitrary"), vmem_limit_bytes=VMEM_LIMIT),
        name="delta",
    )(qkv, gcol, grow, qkv, gcol, grow)


def _gelu(x):
    return 0.5 * x * (1.0 + lax.erf(x * (2.0 ** -0.5)))


def _split_bf16(x):
    hi = x.astype(BF16)
    lo = (x - hi.astype(F32)).astype(BF16)
    return hi, lo


def _mix_kernel(of_ref, ob_ref, z_ref, u_ref, v_ref, x_ref, nw_ref, sg_ref, sb_ref, wsp_ref,
                bsp_ref, wout_ref, lg_ref, lb_ref, wr_ref, rb_ref,
                x1_ref, meta_ref, cnt_ref, *, alpha):
    tm = x_ref.shape[0]

    @pl.when(pl.program_id(0) == 0)
    def _():
        cnt_ref[...] = jnp.zeros_like(cnt_ref)

    o = of_ref[...] + ob_ref[...]
    z = z_ref[...].astype(F32)
    nw = nw_ref[...]
    parts = []
    for h in range(N_HEADS):
        sl = slice(h * HEAD_DIM, (h + 1) * HEAD_DIM)
        seg = o[:, sl]
        seg = seg * lax.rsqrt(jnp.mean(seg * seg, -1, keepdims=True) + RMS_EPS) * nw
        zs = z[:, sl]
        parts.append((seg * (zs * jax.nn.sigmoid(zs))).astype(BF16))

    u = _gelu(u_ref[...].astype(F32))
    v = _layer_norm(_gelu(v_ref[...].astype(F32)), sg_ref[...], sb_ref[...]).astype(BF16)
    for g in range(N_GROUPS_SGU):
        sl = slice(g * HEAD_DIM, (g + 1) * HEAD_DIM)
        w_g = wsp_ref[g]
        blocks = []
        for ch in range(tm // SGU_CHUNK):
            rows = slice(ch * SGU_CHUNK, (ch + 1) * SGU_CHUNK)
            mixed = jnp.dot(w_g, v[rows, sl], preferred_element_type=F32) + bsp_ref[g]
            blocks.append(u[rows, sl] * mixed)
        parts.append(jnp.concatenate(blocks, 0).astype(BF16))

    mix = jnp.dot(jnp.concatenate(parts, -1), wout_ref[...], preferred_element_type=F32)
    x1 = _layer_norm(alpha * x_ref[...] + mix, lg_ref[...], lb_ref[...])
    x1_ref[...] = x1

    x_hi, x_lo = _split_bf16(x1)
    w_hi, w_lo = _split_bf16(wr_ref[...])
    nt = (((1,), (1,)), ((), ()))
    logits = (lax.dot_general(w_hi, x_hi, nt, preferred_element_type=F32)
              + lax.dot_general(w_lo, x_hi, nt, preferred_element_type=F32)
              + lax.dot_general(w_hi, x_lo, nt, preferred_element_type=F32))
    scores = jax.nn.sigmoid(logits)
    sel = scores + rb_ref[:, 0:1]

    def row(m, e):
        return m[e:e + 1, :]

    best = jnp.zeros((1, tm), jnp.int32)
    best_score = None
    for gi in range(N_EXPERT_GROUPS):
        vals = [row(sel, gi * EXPERTS_PER_GROUP + j) for j in range(EXPERTS_PER_GROUP)]
        top2 = None
        for i in range(EXPERTS_PER_GROUP):
            for j in range(i + 1, EXPERTS_PER_GROUP):
                pair = vals[i] + vals[j]
                top2 = pair if top2 is None else jnp.maximum(top2, pair)
        if gi == 0:
            best_score = top2
        else:
            better = top2 > best_score
            best = jnp.where(better, gi, best)
            best_score = jnp.where(better, top2, best_score)

    def pick(m, j):
        out = row(m, j)
        for gi in range(1, N_EXPERT_GROUPS):
            out = jnp.where(best == gi, row(m, gi * EXPERTS_PER_GROUP + j), out)
        return out

    cand = [pick(sel, j) for j in range(EXPERTS_PER_GROUP)]
    raw = [pick(scores, j) for j in range(EXPERTS_PER_GROUP)]
    i1, m1, s1 = jnp.zeros((1, tm), jnp.int32), cand[0], raw[0]
    for j in range(1, EXPERTS_PER_GROUP):
        gt = cand[j] > m1
        i1 = jnp.where(gt, j, i1)
        m1 = jnp.where(gt, cand[j], m1)
        s1 = jnp.where(gt, raw[j], s1)
    i2 = jnp.zeros((1, tm), jnp.int32)
    m2 = jnp.full((1, tm), -jnp.inf, F32)
    s2 = jnp.zeros((1, tm), F32)
    for j in range(EXPERTS_PER_GROUP):
        cj = jnp.where(i1 == j, -jnp.inf, cand[j])
        gt = cj > m2
        i2 = jnp.where(gt, j, i2)
        m2 = jnp.where(gt, cj, m2)
        s2 = jnp.where(gt, raw[j], s2)
    e1 = best * EXPERTS_PER_GROUP + i1
    e2 = best * EXPERTS_PER_GROUP + i2
    denom = s1 + s2
    g1 = s1 / denom
    g2 = s2 / denom

    eidx = lax.broadcasted_iota(jnp.int32, (N_EXPERTS, tm), 0)
    hit1 = eidx == e1
    hit2 = eidx == e2
    cnt = jnp.where(hit1, 1.0, jnp.where(hit2, 1.0, 0.0))
    tr = lax.broadcasted_iota(jnp.int32, (tm, tm), 0)
    tc = lax.broadcasted_iota(jnp.int32, (tm, tm), 1)
    before = jnp.where(tr < tc, 1.0, 0.0).astype(BF16)
    ranks = jnp.dot(cnt.astype(BF16), before, preferred_element_type=F32) + cnt_ref[:, 0:1]
    r1 = jnp.sum(jnp.where(hit1, ranks, 0.0), 0, keepdims=True)
    r2 = jnp.sum(jnp.where(hit2, ranks, 0.0), 0, keepdims=True)
    cnt_ref[...] = cnt_ref[...] + jnp.sum(cnt, 1, keepdims=True)
    meta_ref[...] = jnp.concatenate(
        [e1.astype(F32), e2.astype(F32), r1, r2, g1, g2, jnp.zeros((2, tm), F32)], 0)


def _mix(o_f, o_b, h_main, x2d, norm_w, sgu_g, sgu_b, w_sp, b_sp, w_out, ln_g, ln_b,
         w_router_t, router_bias, alpha, tm):
    n = x2d.shape[0]
    row = lambda i: (i, 0)
    const2 = lambda i: (0, 0)
    const3 = lambda i: (0, 0, 0)
    return pl.pallas_call(
        functools.partial(_mix_kernel, alpha=alpha),
        grid=(n // tm,),
        in_specs=[
            pl.BlockSpec((tm, D_DELTA), row),
            pl.BlockSpec((tm, D_DELTA), row),
            pl.BlockSpec((tm, D_DELTA), lambda i: (i, 3)),
            pl.BlockSpec((tm, D_SGU), lambda i: (i, 4)),
            pl.BlockSpec((tm, D_SGU), lambda i: (i, 5)),
            pl.BlockSpec((tm, D_MODEL), row),
            pl.BlockSpec((1, HEAD_DIM), const2),
            pl.BlockSpec((1, D_SGU), const2),
            pl.BlockSpec((1, D_SGU), const2),
            pl.BlockSpec((N_GROUPS_SGU, SGU_CHUNK, SGU_CHUNK), const3),
            pl.BlockSpec((N_GROUPS_SGU, SGU_CHUNK, HEAD_DIM), const3),
            pl.BlockSpec((D_MODEL, D_MODEL), const2),
            pl.BlockSpec((1, D_MODEL), const2),
            pl.BlockSpec((1, D_MODEL), const2),
            pl.BlockSpec((N_EXPERTS, D_MODEL), const2),
            pl.BlockSpec((N_EXPERTS, 128), const2),
        ],
        out_specs=[
            pl.BlockSpec((tm, D_MODEL), row),
            pl.BlockSpec((8, tm), lambda i: (0, i)),
            pl.BlockSpec((N_EXPERTS, 128), const2),
        ],
        out_shape=[
            jax.ShapeDtypeStruct((n, D_MODEL), F32),
            jax.ShapeDtypeStruct((8, n), F32),
            jax.ShapeDtypeStruct((N_EXPERTS, 128), F32),
        ],
        compiler_params=pltpu.CompilerParams(
            dimension_semantics=("arbitrary",), vmem_limit_bytes=VMEM_LIMIT),
        name="mix",
    )(o_f, o_b, h_main, h_main, h_main, x2d, norm_w, sgu_g, sgu_b, w_sp, b_sp, w_out, ln_g, ln_b,
      w_router_t, router_bias)


def _expert_kernel(te_ref, nt_ref, tok_ref, x_hbm, wg_ref, wu_ref, wd_ref, y_ref, buf, sem):
    i = pl.program_id(0)
    tm = buf.shape[1]
    ntiles = nt_ref[0]

    def gather(tile, slot):
        base = tile * tm

        def body(r, carry):
            tok = tok_ref[base + r]
            pltpu.make_async_copy(x_hbm.at[pl.ds(tok, 1), :], buf.at[slot, pl.ds(r, 1), :],
                                  sem.at[slot]).start()
            return carry

        lax.fori_loop(0, tm, body, 0)

    @pl.when(i == 0)
    def _():
        gather(0, 0)

    @pl.when(i < ntiles)
    def _():
        slot = i % 2

        @pl.when(i + 1 < ntiles)
        def _():
            gather(i + 1, 1 - slot)

        pltpu.make_async_copy(x_hbm.at[pl.ds(0, tm), :], buf.at[slot], sem.at[slot]).wait()
        xs = buf[slot].astype(BF16)
        hg = jnp.dot(xs, wg_ref[...], preferred_element_type=F32)
        hu = jnp.dot(xs, wu_ref[...], preferred_element_type=F32)
        hid = (hg * jax.nn.sigmoid(hg) * hu).astype(BF16)
        y_ref[...] = jnp.dot(hid, wd_ref[...], preferred_element_type=F32)

    @pl.when(i >= ntiles)
    def _():
        y_ref[...] = jnp.zeros_like(y_ref)


def _experts(x1, tile_expert, ntiles, tok_of_slot, w_gate, w_up, w_down, layer, tm):
    max_tiles = tile_expert.shape[0]
    rows = max_tiles * tm

    def wmap(i, te, nt, tok):
        return (layer, te[jnp.minimum(i, nt[0] - 1)], 0, 0)

    wspec = pl.BlockSpec((None, None, D_MODEL, D_MODEL), wmap)
    grid_spec = pltpu.PrefetchScalarGridSpec(
        num_scalar_prefetch=3,
        grid=(max_tiles,),
        in_specs=[pl.BlockSpec(memory_space=pl.ANY), wspec, wspec, wspec],
        out_specs=pl.BlockSpec((tm, D_MODEL), lambda i, te, nt, tok: (i, 0)),
        scratch_shapes=[pltpu.VMEM((2, tm, D_MODEL), F32), pltpu.SemaphoreType.DMA((2,))],
    )
    return pl.pallas_call(
        _expert_kernel,
        grid_spec=grid_spec,
        out_shape=jax.ShapeDtypeStruct((rows, D_MODEL), F32),
        compiler_params=pltpu.CompilerParams(
            dimension_semantics=("arbitrary",), vmem_limit_bytes=VMEM_LIMIT),
        name="experts",
    )(tile_expert, ntiles, tok_of_slot, x1, w_gate, w_up, w_down)


def _combine_kernel(p1_ref, p2_ref, y_hbm, x1_ref, gate_ref, lg_ref, lb_ref, o_ref, buf, sem,
                    *, alpha):
    i = pl.program_id(0)
    n_steps = pl.num_programs(0)
    tm = x1_ref.shape[0]

    def gather(tile, slot):
        base = tile * tm

        def body(r, carry):
            pltpu.make_async_copy(y_hbm.at[pl.ds(p1_ref[base + r], 1), :],
                                  buf.at[slot, 0, pl.ds(r, 1), :], sem.at[slot]).start()
            pltpu.make_async_copy(y_hbm.at[pl.ds(p2_ref[base + r], 1), :],
                                  buf.at[slot, 1, pl.ds(r, 1), :], sem.at[slot]).start()
            return carry

        lax.fori_loop(0, tm, body, 0)

    @pl.when(i == 0)
    def _():
        gather(0, 0)

    slot = i % 2

    @pl.when(i + 1 < n_steps)
    def _():
        gather(i + 1, 1 - slot)

    pltpu.make_async_copy(y_hbm.at[pl.ds(0, tm), :], buf.at[slot, 0], sem.at[slot]).wait()
    pltpu.make_async_copy(y_hbm.at[pl.ds(0, tm), :], buf.at[slot, 1], sem.at[slot]).wait()
    gates = gate_ref[...]
    y = alpha * x1_ref[...] + gates[:, 4:5] * buf[slot, 0] + gates[:, 5:6] * buf[slot, 1]
    o_ref[...] = _layer_norm(y, lg_ref[...], lb_ref[...])


def _combine(pos1, pos2, ys, x1, gates, ln_g, ln_b, alpha, tm):
    n = x1.shape[0]
    grid_spec = pltpu.PrefetchScalarGridSpec(
        num_scalar_prefetch=2,
        grid=(n // tm,),
        in_specs=[
            pl.BlockSpec(memory_space=pl.ANY),
            pl.BlockSpec((tm, D_MODEL), lambda i, p1, p2: (i, 0)),
            pl.BlockSpec((tm, 8), lambda i, p1, p2: (i, 0)),
            pl.BlockSpec((1, D_MODEL), lambda i, p1, p2: (0, 0)),
            pl.BlockSpec((1, D_MODEL), lambda i, p1, p2: (0, 0)),
        ],
        out_specs=pl.BlockSpec((tm, D_MODEL), lambda i, p1, p2: (i, 0)),
        scratch_shapes=[pltpu.VMEM((2, 2, tm, D_MODEL), F32), pltpu.SemaphoreType.DMA((2,))],
    )
    return pl.pallas_call(
        functools.partial(_combine_kernel, alpha=alpha),
        grid_spec=grid_spec,
        out_shape=jax.ShapeDtypeStruct((n, D_MODEL), F32),
        compiler_params=pltpu.CompilerParams(
            dimension_semantics=("arbitrary",), vmem_limit_bytes=VMEM_LIMIT),
        name="combine",
    )(pos1, pos2, ys, x1, gates, ln_g, ln_b)


def _routing_tables(meta, counts, tm_e):
    n = meta.shape[1]
    max_tiles = (2 * n) // tm_e + N_EXPERTS
    cnt = counts[:, 0].astype(jnp.int32)
    tiles = (cnt + tm_e - 1) // tm_e
    tile_end = jnp.cumsum(tiles)
    row_start = (tile_end - tiles) * tm_e
    e1 = meta[0].astype(jnp.int32)
    e2 = meta[1].astype(jnp.int32)
    pos1 = row_start[e1] + meta[2].astype(jnp.int32)
    pos2 = row_start[e2] + meta[3].astype(jnp.int32)
    ntiles = tile_end[-1:]
    tile_ids = jnp.arange(max_tiles, dtype=jnp.int32)
    tile_expert = jnp.minimum(
        jnp.sum((tile_ids[:, None] >= tile_end[None, :]).astype(jnp.int32), 1), N_EXPERTS - 1)
    tok = jnp.arange(n, dtype=jnp.int32)
    tok_of_slot = jnp.zeros((max_tiles * tm_e,), jnp.int32).at[pos1].set(tok).at[pos2].set(tok)
    return pos1, pos2, tile_expert, ntiles.astype(jnp.int32), tok_of_slot


def _forward(x, w_in, conv_w, a_log_f, dt_bias_f, a_log_b, dt_bias_b, delta_norm_w,
             sgu_ln_g, sgu_ln_b, w_spatial, b_spatial, w_out, ln1_g, ln1_b,
             w_router, router_bias, w_gate, w_up, w_down, ln2_g, ln2_b,
             *, tm_proj, tb_conv, tm_mix, tm_exp, tm_comb):
    batch, seq, d = x.shape
    depth = w_in.shape[0]
    n = batch * seq
    alpha = (2.0 * depth) ** 0.25
    x2d = x.reshape(n, d)

    g0 = 4 * D_DELTA
    w_main = jnp.concatenate([w_in[:, :, :g0], w_in[:, :, g0 + 16:]], -1).astype(BF16)
    w_gl = jnp.pad(w_in[:, :, g0:g0 + 16], ((0, 0), (0, 0), (0, GATE_COLS - 16))).astype(BF16)
    conv_w8 = jnp.pad(conv_w, ((0, 0), (0, 8 - CONV_WIDTH), (0, 0)))
    pad_lanes = lambda a, b: jnp.pad(jnp.concatenate([a, b], -1), ((0, 0), (0, GATE_COLS - 8)))
    gate_params = jnp.pad(
        jnp.stack([pad_lanes(a_log_f, a_log_b), pad_lanes(dt_bias_f, dt_bias_b)], 1),
        ((0, 0), (0, 6), (0, 0)))
    w_sp = w_spatial.astype(BF16)
    b_sp = jnp.broadcast_to(b_spatial[..., None], b_spatial.shape + (HEAD_DIM,))
    w_out_b = w_out.astype(BF16)
    w_router_t = w_router.T
    rb = jnp.broadcast_to(router_bias[:, None], (N_EXPERTS, 128))
    w_gate_b, w_up_b, w_down_b = w_gate.astype(BF16), w_up.astype(BF16), w_down.astype(BF16)

    for l in range(depth):
        h_main, gl = _proj(x2d, w_main[l], w_gl[l], tm_proj)
        qkv, gcol = _conv(h_main, gl, conv_w8[l], gate_params[l], seq, tb_conv)
        grow = gcol.reshape(n // CHUNK, CHUNK, 4 * N_HEADS).transpose(0, 2, 1)
        o_f, o_b = _delta(qkv, gcol, grow, batch, seq)
        x1, meta, counts = _mix(
            o_f, o_b, h_main, x2d, delta_norm_w[l][None], sgu_ln_g[l][None], sgu_ln_b[l][None],
            w_sp[l], b_sp[l], w_out_b[l], ln1_g[l][None], ln1_b[l][None], w_router_t, rb,
            alpha, tm_mix)
        pos1, pos2, tile_expert, ntiles, tok_of_slot = _routing_tables(meta, counts, tm_exp)
        ys = _experts(x1, tile_expert, ntiles, tok_of_slot, w_gate_b, w_up_b, w_down_b, l, tm_exp)
        x2d = _combine(pos1, pos2, ys, x1, meta.T, ln2_g[l][None], ln2_b[l][None], alpha, tm_comb)
    return x2d.reshape(batch, seq, d)


def kernel(x, w_in, conv_w, a_log_f, dt_bias_f, a_log_b, dt_bias_b, delta_norm_w, sgu_ln_g,
           sgu_ln_b, w_spatial, b_spatial, w_out, ln1_g, ln1_b, w_router, router_bias, w_gate,
           w_up, w_down, ln2_g, ln2_b):
    return _forward(x, w_in, conv_w, a_log_f, dt_bias_f, a_log_b, dt_bias_b, delta_norm_w,
                    sgu_ln_g, sgu_ln_b, w_spatial, b_spatial, w_out, ln1_g, ln1_b, w_router,
                    router_bias, w_gate, w_up, w_down, ln2_g, ln2_b,
                    tm_proj=512, tb_conv=512, tm_mix=512, tm_exp=512, tm_comb=256)
```

```python
import functools

import jax
import jax.numpy as jnp
from jax import lax
from jax.experimental import pallas as pl
from jax.experimental.pallas import tpu as pltpu

F32 = jnp.float32
BF16 = jnp.bfloat16

D_MODEL = 1024
D_DELTA = 512
D_SGU = 512
HEAD_DIM = 128
N_HEADS = 4
CONV_WIDTH = 5
CHUNK = 64
SGU_CHUNK = 128
N_GROUPS_SGU = 4
N_EXPERTS = 16
N_EXPERT_GROUPS = 4
EXPERTS_PER_GROUP = 4
LN_EPS = 1e-5
RMS_EPS = 1e-6
L2_EPS = 1e-6

MAIN_COLS = 6 * 512
GATE_COLS = 128
HALO_ROWS = 16
VMEM_LIMIT = 56 * 1024 * 1024


def _bdot(a, b):
    return jnp.dot(a.astype(BF16), b.astype(BF16), preferred_element_type=F32)


def _bdot_nt(a, b):
    return lax.dot_general(a.astype(BF16), b.astype(BF16), (((1,), (1,)), ((), ())),
                           preferred_element_type=F32)


def _bdot_tn(a, b):
    return lax.dot_general(a.astype(BF16), b.astype(BF16), (((0,), (0,)), ((), ())),
                           preferred_element_type=F32)


def _layer_norm(y, g, b):
    mu = jnp.mean(y, -1, keepdims=True)
    d = y - mu
    var = jnp.mean(d * d, -1, keepdims=True)
    return d * lax.rsqrt(var + LN_EPS) * g + b


def _proj_kernel(x_ref, w_ref, wg_ref, h_ref, gl_ref, *, col_chunk):
    xb = x_ref[...].astype(BF16)
    for c in range(MAIN_COLS // col_chunk):
        sl = slice(c * col_chunk, (c + 1) * col_chunk)
        h_ref[:, sl] = jnp.dot(xb, w_ref[:, sl], preferred_element_type=F32).astype(BF16)
    gl_ref[...] = jnp.dot(xb, wg_ref[...], preferred_element_type=F32)


def _proj(x2d, w_main, w_gate, tm):
    n = x2d.shape[0]
    return pl.pallas_call(
        functools.partial(_proj_kernel, col_chunk=1024),
        grid=(n // tm,),
        in_specs=[
            pl.BlockSpec((tm, D_MODEL), lambda i: (i, 0)),
            pl.BlockSpec((D_MODEL, MAIN_COLS), lambda i: (0, 0)),
            pl.BlockSpec((D_MODEL, GATE_COLS), lambda i: (0, 0)),
        ],
        out_specs=[
            pl.BlockSpec((tm, MAIN_COLS), lambda i: (i, 0)),
            pl.BlockSpec((tm, GATE_COLS), lambda i: (i, 0)),
        ],
        out_shape=[
            jax.ShapeDtypeStruct((n, MAIN_COLS), BF16),
            jax.ShapeDtypeStruct((n, GATE_COLS), F32),
        ],
        compiler_params=pltpu.CompilerParams(
            dimension_semantics=("parallel",), vmem_limit_bytes=VMEM_LIMIT),
        name="proj",
    )(x2d, w_main, w_gate)


def _conv_taps(ext, w, lo, n):
    rows = ext.shape[0]
    acc = None
    for j in range(CONV_WIDTH):
        shift = (-(j - CONV_WIDTH // 2)) % rows
        r = ext if shift == 0 else pltpu.roll(ext, shift, 0)
        term = r[lo:lo + n] * w[j:j + 1]
        acc = term if acc is None else acc + term
    return acc


def _conv_post(y):
    s = y * jax.nn.sigmoid(y)
    outs = []
    for hh in range(3 * N_HEADS):
        seg = s[:, hh * HEAD_DIM:(hh + 1) * HEAD_DIM]
        if hh < 2 * N_HEADS:
            ss = jnp.sum(seg * seg, -1, keepdims=True)
            seg = seg * lax.rsqrt(ss + L2_EPS)
            if hh < N_HEADS:
                seg = seg * (HEAD_DIM ** -0.5)
        outs.append(seg)
    return jnp.concatenate(outs, -1).astype(BF16)


def _conv_kernel(cur_ref, prev_ref, next_ref, gl_ref, cw_ref, gp_ref, qkv_ref, gcol_ref,
                 *, tiles_per_seq):
    i = pl.program_id(0)
    tb = cur_ref.shape[0]
    pos = i % tiles_per_seq
    cur = cur_ref[...].astype(F32)
    prev = jnp.where(pos == 0, 0.0, prev_ref[...].astype(F32))
    nxt = jnp.where(pos == tiles_per_seq - 1, 0.0, next_ref[...].astype(F32))
    w = cw_ref[...]
    h = HALO_ROWS
    qkv_ref[...] = _conv_post(_conv_taps(cur, w, 0, tb))
    top = _conv_taps(jnp.concatenate([prev, cur[:2 * h]], 0), w, h, h)
    qkv_ref[0:h, :] = _conv_post(top)
    bot = _conv_taps(jnp.concatenate([cur[tb - 2 * h:], nxt], 0), w, h, h)
    qkv_ref[tb - h:tb, :] = _conv_post(bot)

    gl = gl_ref[...]
    neg_a = -jnp.exp(gp_ref[0:1, :])
    xg = gl + gp_ref[1:2, :]
    softplus = jnp.maximum(xg, 0.0) + jnp.log1p(jnp.exp(-jnp.abs(xg)))
    g = neg_a * softplus
    beta = jax.nn.sigmoid(gl)
    lane = lax.broadcasted_iota(jnp.int32, (CHUNK, GATE_COLS), 1)
    r = lax.broadcasted_iota(jnp.int32, (2 * CHUNK, CHUNK), 0)
    c = lax.broadcasted_iota(jnp.int32, (2 * CHUNK, CHUNK), 1)
    tri = jnp.where(r < CHUNK, jnp.where(r >= c, 1.0, 0.0), jnp.where(r - CHUNK <= c, 1.0, 0.0))
    for ch in range(tb // CHUNK):
        sl = slice(ch * CHUNK, (ch + 1) * CHUNK)
        cs = jnp.dot(tri, g[sl], precision=lax.Precision.HIGHEST, preferred_element_type=F32)
        out = jnp.where(lane < N_HEADS, cs[:CHUNK],
                        jnp.where(lane < 2 * N_HEADS, cs[CHUNK:], beta[sl]))
        gcol_ref[sl, :] = out[:, :4 * N_HEADS]


def _conv(h_main, gl, conv_w8, gate_params, seq, tb):
    n = h_main.shape[0]
    hb = tb // HALO_ROWS
    last_hblk = n // HALO_ROWS - 1
    return pl.pallas_call(
        functools.partial(_conv_kernel, tiles_per_seq=seq // tb),
        grid=(n // tb,),
        in_specs=[
            pl.BlockSpec((tb, 3 * D_DELTA), lambda i: (i, 0)),
            pl.BlockSpec((HALO_ROWS, 3 * D_DELTA), lambda i: (jnp.maximum(i * hb - 1, 0), 0)),
            pl.BlockSpec((HALO_ROWS, 3 * D_DELTA),
                         lambda i: (jnp.minimum((i + 1) * hb, last_hblk), 0)),
            pl.BlockSpec((tb, GATE_COLS), lambda i: (i, 0)),
            pl.BlockSpec((8, 3 * D_DELTA), lambda i: (0, 0)),
            pl.BlockSpec((8, GATE_COLS), lambda i: (0, 0)),
        ],
        out_specs=[
            pl.BlockSpec((tb, 3 * D_DELTA), lambda i: (i, 0)),
            pl.BlockSpec((tb, 4 * N_HEADS), lambda i: (i, 0)),
        ],
        out_shape=[
            jax.ShapeDtypeStruct((n, 3 * D_DELTA), BF16),
            jax.ShapeDtypeStruct((n, 4 * N_HEADS), F32),
        ],
        compiler_params=pltpu.CompilerParams(
            dimension_semantics=("parallel",), vmem_limit_bytes=VMEM_LIMIT),
        name="conv",
    )(h_main, h_main, h_main, gl, conv_w8, gate_params)


def _delta_chains(chains):
    c, nh = CHUNK, N_HEADS
    hc = nh * c
    n_ch = len(chains)
    every = range(n_ch)

    ri = lax.broadcasted_iota(jnp.int32, (hc, hc), 0)
    ci = lax.broadcasted_iota(jnp.int32, (hc, hc), 1)
    same64 = (ri >> 6) == (ci >> 6)
    same32 = (ri >> 5) == (ci >> 5)
    same16 = (ri >> 4) == (ci >> 4)
    eye = jnp.where(ri == ci, 1.0, 0.0)
    incl_of = {True: same64 & (ri >= ci), False: same64 & (ri <= ci)}
    strict_of = {d: m & (ri != ci) for d, m in incl_of.items()}

    q, k, v, gc_col, beta, g_last, k_dec, qg, decay, incl = ([] for _ in range(10))
    for qkv, gcol, grow, _, _, fwd in chains:
        off = 0 if fwd else nh

        def stack(base):
            return jnp.concatenate(
                [qkv[:, (base + h) * HEAD_DIM:(base + h + 1) * HEAD_DIM] for h in range(nh)], 0
            ).astype(F32)

        q.append(stack(0))
        k.append(stack(nh))
        v.append(stack(2 * nh))
        col = jnp.concatenate([gcol[:, off + h:off + h + 1] for h in range(nh)], 0)
        row = jnp.concatenate([grow[off + h:off + h + 1, :] for h in range(nh)], 1)
        gc_col.append(col)
        beta.append(jnp.concatenate(
            [gcol[:, 2 * nh + off + h:2 * nh + off + h + 1] for h in range(nh)], 0))
        edge = c - 1 if fwd else 0
        gl = [gcol[edge:edge + 1, off + h:off + h + 1] for h in range(nh)]
        g_last.append(gl)
        gl_col = jnp.concatenate([jnp.broadcast_to(g, (c, 1)) for g in gl], 0)
        k_dec.append(k[-1] * jnp.exp(gl_col - col))
        qg.append(q[-1] * jnp.exp(col))
        m = incl_of[fwd]
        incl.append(m)
        decay.append(jnp.where(m, jnp.exp(jnp.where(m, col - row, 0.0)), 0.0))

    kb = [k[i] * beta[i] for i in every]
    a = [jnp.where(strict_of[chains[i][5]], _bdot_nt(kb[i], k[i]) * decay[i], 0.0) for i in every]
    qk = [jnp.where(incl[i], _bdot_nt(q[i], k[i]) * decay[i], 0.0) for i in every]

    p = [jnp.where(same16, a[i], 0.0) for i in every]
    t = [eye - p[i] for i in every]
    for _ in range(3):
        p = [_bdot(p[i], p[i]) for i in every]
        t = [t[i] + _bdot(t[i], p[i]) for i in every]
    m = [_bdot(jnp.where(same32, jnp.where(same16, 0.0, a[i]), 0.0), t[i]) for i in every]
    t = [t[i] - _bdot(t[i], m[i]) for i in every]
    m = [_bdot(jnp.where(same32, 0.0, a[i]), t[i]) for i in every]
    t = [t[i] - _bdot(t[i], m[i]) for i in every]

    uw = [_bdot(t[i], jnp.concatenate([v[i] * beta[i], kb[i] * jnp.exp(gc_col[i])], 1))
          for i in every]

    ws = [[_bdot(jnp.concatenate([uw[i][h * c:(h + 1) * c, HEAD_DIM:],
                                  qg[i][h * c:(h + 1) * c]], 0), chains[i][3][h])
           for h in range(nh)] for i in every]
    v_new = [[uw[i][h * c:(h + 1) * c, :HEAD_DIM] - ws[i][h][:c] for h in range(nh)]
             for i in every]
    o = [jnp.concatenate([ws[i][h][c:] for h in range(nh)], 0)
         + _bdot(qk[i], jnp.concatenate(v_new[i], 0)) for i in every]
    for i in every:
        s_ref, o_ref = chains[i][3], chains[i][4]
        for h in range(nh):
            rows = slice(h * c, (h + 1) * c)
            s_ref[h] = (s_ref[h] * jnp.exp(g_last[i][h])
                        + _bdot_tn(k_dec[i][rows], v_new[i][h]))
            o_ref[:, h * HEAD_DIM:(h + 1) * HEAD_DIM] = o[i][rows]


def _delta_kernel(qkv_f_ref, gcol_f_ref, grow_f_ref, qkv_b_ref, gcol_b_ref, grow_b_ref,
                  of_ref, ob_ref, sf_ref, sb_ref):
    @pl.when(pl.program_id(1) == 0)
    def _():
        sf_ref[...] = jnp.zeros_like(sf_ref)
        sb_ref[...] = jnp.zeros_like(sb_ref)

    chains = []
    for bb in range(qkv_f_ref.shape[0]):
        chains.append((qkv_f_ref[bb], gcol_f_ref[bb], grow_f_ref[bb, 0], sf_ref.at[bb],
                       of_ref.at[bb], True))
        chains.append((qkv_b_ref[bb], gcol_b_ref[bb], grow_b_ref[bb, 0], sb_ref.at[bb],
                       ob_ref.at[bb], False))
    _delta_chains(chains)


def _delta(qkv, gcol, grow, batch, seq, bb):
    nc = seq // CHUNK
    fwd = lambda b, j: (b, j, 0)
    bwd = lambda b, j: (b, nc - 1 - j, 0)
    fwd4 = lambda b, j: (b, j, 0, 0)
    bwd4 = lambda b, j: (b, nc - 1 - j, 0, 0)
    return pl.pallas_call(
        _delta_kernel,
        grid=(batch // bb, nc),
        in_specs=[
            pl.BlockSpec((bb, CHUNK, 3 * D_DELTA), fwd),
            pl.BlockSpec((bb, CHUNK, 4 * N_HEADS), fwd),
            pl.BlockSpec((bb, 1, 4 * N_HEADS, CHUNK), fwd4),
            pl.BlockSpec((bb, CHUNK, 3 * D_DELTA), bwd),
            pl.BlockSpec((bb, CHUNK, 4 * N_HEADS), bwd),
            pl.BlockSpec((bb, 1, 4 * N_HEADS, CHUNK), bwd4),
        ],
        out_specs=[
            pl.BlockSpec((bb, CHUNK, D_DELTA), fwd),
            pl.BlockSpec((bb, CHUNK, D_DELTA), bwd),
        ],
        out_shape=[
            jax.ShapeDtypeStruct((batch, seq, D_DELTA), F32),
            jax.ShapeDtypeStruct((batch, seq, D_DELTA), F32),
        ],
        scratch_shapes=[
            pltpu.VMEM((bb, N_HEADS, HEAD_DIM, HEAD_DIM), F32),
            pltpu.VMEM((bb, N_HEADS, HEAD_DIM, HEAD_DIM), F32),
        ],
        compiler_params=pltpu.CompilerParams(
            dimension_semantics=("arbitrary", "arbitrary"), vmem_limit_bytes=VMEM_LIMIT),
        name="delta",
    )(qkv, gcol, grow, qkv, gcol, grow)


def _gelu(x):
    return 0.5 * x * (1.0 + lax.erf(x * (2.0 ** -0.5)))


def _split_bf16(x):
    hi = x.astype(BF16)
    lo = (x - hi.astype(F32)).astype(BF16)
    return hi, lo


def _mix_kernel(of_ref, ob_ref, z_ref, u_ref, v_ref, x_ref, nw_ref, sg_ref, sb_ref, wsp_ref,
                bsp_ref, wout_ref, lg_ref, lb_ref, wr_ref, rb_ref,
                x1_ref, meta_ref, cnt_ref, *, alpha):
    tm = x_ref.shape[0]

    @pl.when(pl.program_id(0) == 0)
    def _():
        cnt_ref[...] = jnp.zeros_like(cnt_ref)

    o = of_ref[...] + ob_ref[...]
    z = z_ref[...].astype(F32)
    nw = nw_ref[...]
    parts = []
    for h in range(N_HEADS):
        sl = slice(h * HEAD_DIM, (h + 1) * HEAD_DIM)
        seg = o[:, sl]
        seg = seg * lax.rsqrt(jnp.mean(seg * seg, -1, keepdims=True) + RMS_EPS) * nw
        zs = z[:, sl]
        parts.append((seg * (zs * jax.nn.sigmoid(zs))).astype(BF16))

    u = _gelu(u_ref[...].astype(F32))
    v = _layer_norm(_gelu(v_ref[...].astype(F32)), sg_ref[...], sb_ref[...]).astype(BF16)
    for g in range(N_GROUPS_SGU):
        sl = slice(g * HEAD_DIM, (g + 1) * HEAD_DIM)
        w_g = wsp_ref[g]
        blocks = []
        for ch in range(tm // SGU_CHUNK):
            rows = slice(ch * SGU_CHUNK, (ch + 1) * SGU_CHUNK)
            mixed = jnp.dot(w_g, v[rows, sl], preferred_element_type=F32) + bsp_ref[g]
            blocks.append(u[rows, sl] * mixed)
        parts.append(jnp.concatenate(blocks, 0).astype(BF16))

    mix = jnp.dot(jnp.concatenate(parts, -1), wout_ref[...], preferred_element_type=F32)
    x1 = _layer_norm(alpha * x_ref[...] + mix, lg_ref[...], lb_ref[...])
    x1_ref[...] = x1

    x_hi, x_lo = _split_bf16(x1)
    w_hi, w_lo = _split_bf16(wr_ref[...])
    nt = (((1,), (1,)), ((), ()))
    logits = (lax.dot_general(w_hi, x_hi, nt, preferred_element_type=F32)
              + lax.dot_general(w_lo, x_hi, nt, preferred_element_type=F32)
              + lax.dot_general(w_hi, x_lo, nt, preferred_element_type=F32))
    scores = jax.nn.sigmoid(logits)
    sel = scores + rb_ref[:, 0:1]

    def row(m, e):
        return m[e:e + 1, :]

    best = jnp.zeros((1, tm), jnp.int32)
    best_score = None
    for gi in range(N_EXPERT_GROUPS):
        vals = [row(sel, gi * EXPERTS_PER_GROUP + j) for j in range(EXPERTS_PER_GROUP)]
        top2 = None
        for i in range(EXPERTS_PER_GROUP):
            for j in range(i + 1, EXPERTS_PER_GROUP):
                pair = vals[i] + vals[j]
                top2 = pair if top2 is None else jnp.maximum(top2, pair)
        if gi == 0:
            best_score = top2
        else:
            better = top2 > best_score
            best = jnp.where(better, gi, best)
            best_score = jnp.where(better, top2, best_score)

    def pick(m, j):
        out = row(m, j)
        for gi in range(1, N_EXPERT_GROUPS):
            out = jnp.where(best == gi, row(m, gi * EXPERTS_PER_GROUP + j), out)
        return out

    cand = [pick(sel, j) for j in range(EXPERTS_PER_GROUP)]
    raw = [pick(scores, j) for j in range(EXPERTS_PER_GROUP)]
    i1, m1, s1 = jnp.zeros((1, tm), jnp.int32), cand[0], raw[0]
    for j in range(1, EXPERTS_PER_GROUP):
        gt = cand[j] > m1
        i1 = jnp.where(gt, j, i1)
        m1 = jnp.where(gt, cand[j], m1)
        s1 = jnp.where(gt, raw[j], s1)
    i2 = jnp.zeros((1, tm), jnp.int32)
    m2 = jnp.full((1, tm), -jnp.inf, F32)
    s2 = jnp.zeros((1, tm), F32)
    for j in range(EXPERTS_PER_GROUP):
        cj = jnp.where(i1 == j, -jnp.inf, cand[j])
        gt = cj > m2
        i2 = jnp.where(gt, j, i2)
        m2 = jnp.where(gt, cj, m2)
        s2 = jnp.where(gt, raw[j], s2)
    e1 = best * EXPERTS_PER_GROUP + i1
    e2 = best * EXPERTS_PER_GROUP + i2
    denom = s1 + s2
    g1 = s1 / denom
    g2 = s2 / denom

    eidx = lax.broadcasted_iota(jnp.int32, (N_EXPERTS, tm), 0)
    hit1 = eidx == e1
    hit2 = eidx == e2
    cnt = jnp.where(hit1, 1.0, jnp.where(hit2, 1.0, 0.0))
    tr = lax.broadcasted_iota(jnp.int32, (tm, tm), 0)
    tc = lax.broadcasted_iota(jnp.int32, (tm, tm), 1)
    before = jnp.where(tr < tc, 1.0, 0.0).astype(BF16)
    ranks = jnp.dot(cnt.astype(BF16), before, preferred_element_type=F32) + cnt_ref[:, 0:1]
    r1 = jnp.sum(jnp.where(hit1, ranks, 0.0), 0, keepdims=True)
    r2 = jnp.sum(jnp.where(hit2, ranks, 0.0), 0, keepdims=True)
    cnt_ref[...] = cnt_ref[...] + jnp.sum(cnt, 1, keepdims=True)
    meta_ref[...] = jnp.concatenate(
        [e1.astype(F32), e2.astype(F32), r1, r2, g1, g2, jnp.zeros((2, tm), F32)], 0)


def _mix(o_f, o_b, h_main, x2d, norm_w, sgu_g, sgu_b, w_sp, b_sp, w_out, ln_g, ln_b,
         w_router_t, router_bias, alpha, tm):
    n = x2d.shape[0]
    row = lambda i: (i, 0)
    const2 = lambda i: (0, 0)
    const3 = lambda i: (0, 0, 0)
    return pl.pallas_call(
        functools.partial(_mix_kernel, alpha=alpha),
        grid=(n // tm,),
        in_specs=[
            pl.BlockSpec((tm, D_DELTA), row),
            pl.BlockSpec((tm, D_DELTA), row),
            pl.BlockSpec((tm, D_DELTA), lambda i: (i, 3)),
            pl.BlockSpec((tm, D_SGU), lambda i: (i, 4)),
            pl.BlockSpec((tm, D_SGU), lambda i: (i, 5)),
            pl.BlockSpec((tm, D_MODEL), row),
            pl.BlockSpec((1, HEAD_DIM), const2),
            pl.BlockSpec((1, D_SGU), const2),
            pl.BlockSpec((1, D_SGU), const2),
            pl.BlockSpec((N_GROUPS_SGU, SGU_CHUNK, SGU_CHUNK), const3),
            pl.BlockSpec((N_GROUPS_SGU, SGU_CHUNK, HEAD_DIM), const3),
            pl.BlockSpec((D_MODEL, D_MODEL), const2),
            pl.BlockSpec((1, D_MODEL), const2),
            pl.BlockSpec((1, D_MODEL), const2),
            pl.BlockSpec((N_EXPERTS, D_MODEL), const2),
            pl.BlockSpec((N_EXPERTS, 128), const2),
        ],
        out_specs=[
            pl.BlockSpec((tm, D_MODEL), row),
            pl.BlockSpec((8, tm), lambda i: (0, i)),
            pl.BlockSpec((N_EXPERTS, 128), const2),
        ],
        out_shape=[
            jax.ShapeDtypeStruct((n, D_MODEL), F32),
            jax.ShapeDtypeStruct((8, n), F32),
            jax.ShapeDtypeStruct((N_EXPERTS, 128), F32),
        ],
        compiler_params=pltpu.CompilerParams(
            dimension_semantics=("arbitrary",), vmem_limit_bytes=VMEM_LIMIT),
        name="mix",
    )(o_f, o_b, h_main, h_main, h_main, x2d, norm_w, sgu_g, sgu_b, w_sp, b_sp, w_out, ln_g, ln_b,
      w_router_t, router_bias)


def _expert_kernel(te_ref, nt_ref, tok_ref, x_hbm, wg_ref, wu_ref, wd_ref, y_ref, buf, sem):
    i = pl.program_id(0)
    tm = buf.shape[1]
    ntiles = nt_ref[0]

    def gather(tile, slot):
        base = tile * tm
        for r in range(tm):
            pltpu.make_async_copy(x_hbm.at[pl.ds(tok_ref[base + r], 1), :],
                                  buf.at[slot, pl.ds(r, 1), :], sem.at[slot]).start()

    def wait(slot):
        pltpu.make_async_copy(x_hbm.at[pl.ds(0, tm), :], buf.at[slot], sem.at[slot]).wait()

    @pl.when(i == 0)
    def _():
        gather(0, 0)

    def step(slot):
        wait(slot)
        gather(jnp.minimum(i + 1, ntiles - 1), 1 - slot)
        xs = buf[slot].astype(BF16)
        hg = jnp.dot(xs, wg_ref[...], preferred_element_type=F32)
        hu = jnp.dot(xs, wu_ref[...], preferred_element_type=F32)
        hid = (hg * jax.nn.sigmoid(hg) * hu).astype(BF16)
        y_ref[...] = jnp.dot(hid, wd_ref[...], preferred_element_type=F32)

        @pl.when(i == ntiles - 1)
        def _():
            wait(1 - slot)

    for parity in range(2):
        pl.when((i < ntiles) & (i % 2 == parity))(functools.partial(step, parity))

    @pl.when(i >= ntiles)
    def _():
        y_ref[...] = jnp.zeros_like(y_ref)


def _experts(x1, tile_expert, ntiles, tok_of_slot, w_gate, w_up, w_down, layer, tm):
    max_tiles = tile_expert.shape[0]
    rows = max_tiles * tm

    def wmap(i, te, nt, tok):
        return (layer, te[jnp.minimum(i, nt[0] - 1)], 0, 0)

    wspec = pl.BlockSpec((None, None, D_MODEL, D_MODEL), wmap)
    grid_spec = pltpu.PrefetchScalarGridSpec(
        num_scalar_prefetch=3,
        grid=(max_tiles,),
        in_specs=[pl.BlockSpec(memory_space=pl.ANY), wspec, wspec, wspec],
        out_specs=pl.BlockSpec((tm, D_MODEL), lambda i, te, nt, tok: (i, 0)),
        scratch_shapes=[pltpu.VMEM((2, tm, D_MODEL), F32), pltpu.SemaphoreType.DMA((2,))],
    )
    return pl.pallas_call(
        _expert_kernel,
        grid_spec=grid_spec,
        out_shape=jax.ShapeDtypeStruct((rows, D_MODEL), F32),
        compiler_params=pltpu.CompilerParams(
            dimension_semantics=("arbitrary",), vmem_limit_bytes=VMEM_LIMIT),
        name="experts",
    )(tile_expert, ntiles, tok_of_slot, x1, w_gate, w_up, w_down)


def _combine_kernel(p1_ref, p2_ref, y_hbm, x1_ref, gate_ref, lg_ref, lb_ref, o_ref, buf, sem,
                    *, alpha):
    i = pl.program_id(0)
    n_steps = pl.num_programs(0)
    tm = x1_ref.shape[0]

    def gather(tile, slot):
        base = tile * tm
        for r in range(tm):
            pltpu.make_async_copy(y_hbm.at[pl.ds(p1_ref[base + r], 1), :],
                                  buf.at[slot, 0, pl.ds(r, 1), :], sem.at[slot]).start()
            pltpu.make_async_copy(y_hbm.at[pl.ds(p2_ref[base + r], 1), :],
                                  buf.at[slot, 1, pl.ds(r, 1), :], sem.at[slot]).start()

    def wait(slot):
        pltpu.make_async_copy(y_hbm.at[pl.ds(0, tm), :], buf.at[slot, 0], sem.at[slot]).wait()
        pltpu.make_async_copy(y_hbm.at[pl.ds(0, tm), :], buf.at[slot, 1], sem.at[slot]).wait()

    @pl.when(i == 0)
    def _():
        gather(0, 0)

    def step(slot):
        wait(slot)
        gather(jnp.minimum(i + 1, n_steps - 1), 1 - slot)
        gates = gate_ref[...]
        y = alpha * x1_ref[...] + gates[:, 4:5] * buf[slot, 0] + gates[:, 5:6] * buf[slot, 1]
        o_ref[...] = _layer_norm(y, lg_ref[...], lb_ref[...])

        @pl.when(i == n_steps - 1)
        def _():
            wait(1 - slot)

    for parity in range(2):
        pl.when(i % 2 == parity)(functools.partial(step, parity))


def _combine(pos1, pos2, ys, x1, gates, ln_g, ln_b, alpha, tm):
    n = x1.shape[0]
    grid_spec = pltpu.PrefetchScalarGridSpec(
        num_scalar_prefetch=2,
        grid=(n // tm,),
        in_specs=[
            pl.BlockSpec(memory_space=pl.ANY),
            pl.BlockSpec((tm, D_MODEL), lambda i, p1, p2: (i, 0)),
            pl.BlockSpec((tm, 8), lambda i, p1, p2: (i, 0)),
            pl.BlockSpec((1, D_MODEL), lambda i, p1, p2: (0, 0)),
            pl.BlockSpec((1, D_MODEL), lambda i, p1, p2: (0, 0)),
        ],
        out_specs=pl.BlockSpec((tm, D_MODEL), lambda i, p1, p2: (i, 0)),
        scratch_shapes=[pltpu.VMEM((2, 2, tm, D_MODEL), F32), pltpu.SemaphoreType.DMA((2,))],
    )
    return pl.pallas_call(
        functools.partial(_combine_kernel, alpha=alpha),
        grid_spec=grid_spec,
        out_shape=jax.ShapeDtypeStruct((n, D_MODEL), F32),
        compiler_params=pltpu.CompilerParams(
            dimension_semantics=("arbitrary",), vmem_limit_bytes=VMEM_LIMIT),
        name="combine",
    )(pos1, pos2, ys, x1, gates, ln_g, ln_b)


def _routing_tables(meta, counts, tm_e):
    n = meta.shape[1]
    max_tiles = (2 * n) // tm_e + N_EXPERTS
    cnt = counts[:, 0].astype(jnp.int32)
    tiles = (cnt + tm_e - 1) // tm_e
    tile_end = jnp.cumsum(tiles)
    row_start = (tile_end - tiles) * tm_e
    e1 = meta[0].astype(jnp.int32)
    e2 = meta[1].astype(jnp.int32)
    pos1 = row_start[e1] + meta[2].astype(jnp.int32)
    pos2 = row_start[e2] + meta[3].astype(jnp.int32)
    ntiles = tile_end[-1:]
    tile_ids = jnp.arange(max_tiles, dtype=jnp.int32)
    tile_expert = jnp.minimum(
        jnp.sum((tile_ids[:, None] >= tile_end[None, :]).astype(jnp.int32), 1), N_EXPERTS - 1)
    tok = jnp.arange(n, dtype=jnp.int32)
    tok_of_slot = jnp.zeros((max_tiles * tm_e,), jnp.int32).at[pos1].set(tok).at[pos2].set(tok)
    return pos1, pos2, tile_expert, ntiles.astype(jnp.int32), tok_of_slot


def _forward(x, w_in, conv_w, a_log_f, dt_bias_f, a_log_b, dt_bias_b, delta_norm_w,
             sgu_ln_g, sgu_ln_b, w_spatial, b_spatial, w_out, ln1_g, ln1_b,
             w_router, router_bias, w_gate, w_up, w_down, ln2_g, ln2_b,
             *, tm_proj, tb_conv, bb_delta, tm_mix, tm_exp, tm_comb):
    batch, seq, d = x.shape
    depth = w_in.shape[0]
    n = batch * seq
    alpha = (2.0 * depth) ** 0.25
    x2d = x.reshape(n, d)

    g0 = 4 * D_DELTA
    w_main = jnp.concatenate([w_in[:, :, :g0], w_in[:, :, g0 + 16:]], -1).astype(BF16)
    w_gl = jnp.pad(w_in[:, :, g0:g0 + 16], ((0, 0), (0, 0), (0, GATE_COLS - 16))).astype(BF16)
    conv_w8 = jnp.pad(conv_w, ((0, 0), (0, 8 - CONV_WIDTH), (0, 0)))
    pad_lanes = lambda a, b: jnp.pad(jnp.concatenate([a, b], -1), ((0, 0), (0, GATE_COLS - 8)))
    gate_params = jnp.pad(
        jnp.stack([pad_lanes(a_log_f, a_log_b), pad_lanes(dt_bias_f, dt_bias_b)], 1),
        ((0, 0), (0, 6), (0, 0)))
    w_sp = w_spatial.astype(BF16)
    b_sp = jnp.broadcast_to(b_spatial[..., None], b_spatial.shape + (HEAD_DIM,))
    w_out_b = w_out.astype(BF16)
    w_router_t = w_router.T
    rb = jnp.broadcast_to(router_bias[:, None], (N_EXPERTS, 128))
    w_gate_b, w_up_b, w_down_b = w_gate.astype(BF16), w_up.astype(BF16), w_down.astype(BF16)

    for l in range(depth):
        h_main, gl = _proj(x2d, w_main[l], w_gl[l], tm_proj)
        qkv, gcol = _conv(h_main, gl, conv_w8[l], gate_params[l], seq, tb_conv)
        gcol3 = gcol.reshape(batch, seq, 4 * N_HEADS)
        grow = gcol.reshape(batch, seq // CHUNK, CHUNK, 4 * N_HEADS).transpose(0, 1, 3, 2)
        o_f, o_b = _delta(qkv.reshape(batch, seq, 3 * D_DELTA), gcol3, grow, batch, seq, bb_delta)
        o_f, o_b = o_f.reshape(n, D_DELTA), o_b.reshape(n, D_DELTA)
        x1, meta, counts = _mix(
            o_f, o_b, h_main, x2d, delta_norm_w[l][None], sgu_ln_g[l][None], sgu_ln_b[l][None],
            w_sp[l], b_sp[l], w_out_b[l], ln1_g[l][None], ln1_b[l][None], w_router_t, rb,
            alpha, tm_mix)
        pos1, pos2, tile_expert, ntiles, tok_of_slot = _routing_tables(meta, counts, tm_exp)
        ys = _experts(x1, tile_expert, ntiles, tok_of_slot, w_gate_b, w_up_b, w_down_b, l, tm_exp)
        x2d = _combine(pos1, pos2, ys, x1, meta.T, ln2_g[l][None], ln2_b[l][None], alpha, tm_comb)
    return x2d.reshape(batch, seq, d)


def kernel(x, w_in, conv_w, a_log_f, dt_bias_f, a_log_b, dt_bias_b, delta_norm_w, sgu_ln_g,
           sgu_ln_b, w_spatial, b_spatial, w_out, ln1_g, ln1_b, w_router, router_bias, w_gate,
           w_up, w_down, ln2_g, ln2_b):
    return _forward(x, w_in, conv_w, a_log_f, dt_bias_f, a_log_b, dt_bias_b, delta_norm_w,
                    sgu_ln_g, sgu_ln_b, w_spatial, b_spatial, w_out, ln1_g, ln1_b, w_router,
                    router_bias, w_gate, w_up, w_down, ln2_g, ln2_b,
                    tm_proj=512, tb_conv=512, bb_delta=4, tm_mix=512, tm_exp=512, tm_comb=256)
```

```python
import functools

import jax
import jax.numpy as jnp
from jax import lax
from jax.experimental import pallas as pl
from jax.experimental.pallas import tpu as pltpu

F32 = jnp.float32
BF16 = jnp.bfloat16

D_MODEL = 1024
D_DELTA = 512
D_SGU = 512
HEAD_DIM = 128
N_HEADS = 4
CONV_WIDTH = 5
CHUNK = 64
SGU_CHUNK = 128
N_GROUPS_SGU = 4
N_EXPERTS = 16
N_EXPERT_GROUPS = 4
EXPERTS_PER_GROUP = 4
LN_EPS = 1e-5
RMS_EPS = 1e-6
L2_EPS = 1e-6

GATE_COLS = 128
HALO_ROWS = 16
LANES = 128
SLAB = D_MODEL // LANES
VMEM_LIMIT = 56 * 1024 * 1024


def _bdot(a, b):
    return jnp.dot(a.astype(BF16), b.astype(BF16), preferred_element_type=F32)


def _bdot_nt(a, b):
    return lax.dot_general(a.astype(BF16), b.astype(BF16), (((1,), (1,)), ((), ())),
                           preferred_element_type=F32)


def _bdot_tn(a, b):
    return lax.dot_general(a.astype(BF16), b.astype(BF16), (((0,), (0,)), ((), ())),
                           preferred_element_type=F32)


def _layer_norm(y, g, b):
    mu = jnp.mean(y, -1, keepdims=True)
    d = y - mu
    var = jnp.mean(d * d, -1, keepdims=True)
    return d * lax.rsqrt(var + LN_EPS) * g + b


def _load_slabs(ref, rows):
    return [ref[pl.ds(s, rows, stride=SLAB), :] for s in range(SLAB)]


def _store_slabs(ref, mat):
    rows = mat.shape[0]
    for s in range(SLAB):
        ref[pl.ds(s, rows, stride=SLAB), :] = mat[:, s * LANES:(s + 1) * LANES]


def _conv_taps(ext, w, lo, n):
    rows = ext.shape[0]
    acc = None
    for j in range(CONV_WIDTH):
        shift = (-(j - CONV_WIDTH // 2)) % rows
        r = ext if shift == 0 else pltpu.roll(ext, shift, 0)
        term = r[lo:lo + n] * w[j:j + 1]
        acc = term if acc is None else acc + term
    return acc


def _conv_post(y, first_head):
    s = y * jax.nn.sigmoid(y)
    outs = []
    for j in range(y.shape[1] // HEAD_DIM):
        hh = first_head + j
        seg = s[:, j * HEAD_DIM:(j + 1) * HEAD_DIM]
        if hh < 2 * N_HEADS:
            ss = jnp.sum(seg * seg, -1, keepdims=True)
            seg = seg * lax.rsqrt(ss + L2_EPS)
            if hh < N_HEADS:
                seg = seg * (HEAD_DIM ** -0.5)
        outs.append(seg)
    return jnp.concatenate(outs, -1).astype(BF16)


def _proj_kernel(cur_ref, prev_ref, next_ref, wqkv_ref, wzuv_ref, wgl_ref, cw_ref, gp_ref,
                 qkv_ref, zuv_ref, gcol_ref, xe_ref, *, tiles_per_seq, col_chunk):
    i = pl.program_id(0)
    tb = cur_ref.shape[0]
    h = HALO_ROWS
    pos = i % tiles_per_seq
    xc = cur_ref[...].astype(BF16)
    xp = jnp.where(pos == 0, 0.0, prev_ref[...]).astype(BF16)
    xn = jnp.where(pos == tiles_per_seq - 1, 0.0, next_ref[...]).astype(BF16)
    xe_ref[0:h, :] = xp
    xe_ref[h:h + tb, :] = xc
    xe_ref[h + tb:, :] = xn
    n_chunks = 3 * D_DELTA // col_chunk
    cols = [slice(c * col_chunk, (c + 1) * col_chunk) for c in range(n_chunks)]

    def conv(c, h_ext):
        y = _conv_taps(h_ext, cw_ref[:, cols[c]], h, tb)
        qkv_ref[:, cols[c]] = _conv_post(y, c * col_chunk // HEAD_DIM)

    h_prev = jnp.dot(xe_ref[...], wqkv_ref[:, cols[0]], preferred_element_type=F32)
    for c in range(1, n_chunks):
        h_cur = jnp.dot(xe_ref[...], wqkv_ref[:, cols[c]], preferred_element_type=F32)
        conv(c - 1, h_prev)
        h_prev = h_cur
    for c in range(n_chunks):
        zuv_ref[:, cols[c]] = jnp.dot(xe_ref[h:h + tb, :], wzuv_ref[:, cols[c]],
                                      preferred_element_type=F32).astype(BF16)
        if c == 0:
            conv(n_chunks - 1, h_prev)

    gl = jnp.dot(xe_ref[h:h + tb, :], wgl_ref[...], preferred_element_type=F32)
    neg_a = -jnp.exp(gp_ref[0:1, :])
    xg = gl + gp_ref[1:2, :]
    softplus = jnp.maximum(xg, 0.0) + jnp.log1p(jnp.exp(-jnp.abs(xg)))
    g = neg_a * softplus
    beta = jax.nn.sigmoid(gl)
    lane = lax.broadcasted_iota(jnp.int32, (CHUNK, GATE_COLS), 1)
    r = lax.broadcasted_iota(jnp.int32, (2 * CHUNK, CHUNK), 0)
    c = lax.broadcasted_iota(jnp.int32, (2 * CHUNK, CHUNK), 1)
    tri = jnp.where(r < CHUNK, jnp.where(r >= c, 1.0, 0.0), jnp.where(r - CHUNK <= c, 1.0, 0.0))
    for ch in range(tb // CHUNK):
        sl = slice(ch * CHUNK, (ch + 1) * CHUNK)
        cs = jnp.dot(tri, g[sl], precision=lax.Precision.HIGHEST, preferred_element_type=F32)
        out = jnp.where(lane < N_HEADS, cs[:CHUNK],
                        jnp.where(lane < 2 * N_HEADS, cs[CHUNK:], beta[sl]))
        gcol_ref[sl, :] = out[:, :4 * N_HEADS]


def _proj(x2d, w_qkv, w_zuv, w_gl, conv_w8, gate_params, seq, tb):
    n = x2d.shape[0]
    hb = tb // HALO_ROWS
    last_hblk = n // HALO_ROWS - 1
    const = lambda i: (0, 0)
    return pl.pallas_call(
        functools.partial(_proj_kernel, tiles_per_seq=seq // tb, col_chunk=4 * HEAD_DIM),
        grid=(n // tb,),
        in_specs=[
            pl.BlockSpec((tb, D_MODEL), lambda i: (i, 0)),
            pl.BlockSpec((HALO_ROWS, D_MODEL), lambda i: (jnp.maximum(i * hb - 1, 0), 0)),
            pl.BlockSpec((HALO_ROWS, D_MODEL), lambda i: (jnp.minimum((i + 1) * hb, last_hblk), 0)),
            pl.BlockSpec((D_MODEL, 3 * D_DELTA), const),
            pl.BlockSpec((D_MODEL, 3 * D_DELTA), const),
            pl.BlockSpec((D_MODEL, GATE_COLS), const),
            pl.BlockSpec((8, 3 * D_DELTA), const),
            pl.BlockSpec((8, GATE_COLS), const),
        ],
        out_specs=[
            pl.BlockSpec((tb, 3 * D_DELTA), lambda i: (i, 0)),
            pl.BlockSpec((tb, 3 * D_DELTA), lambda i: (i, 0)),
            pl.BlockSpec((tb, 4 * N_HEADS), lambda i: (i, 0)),
        ],
        out_shape=[
            jax.ShapeDtypeStruct((n, 3 * D_DELTA), BF16),
            jax.ShapeDtypeStruct((n, 3 * D_DELTA), BF16),
            jax.ShapeDtypeStruct((n, 4 * N_HEADS), F32),
        ],
        scratch_shapes=[pltpu.VMEM((tb + 2 * HALO_ROWS, D_MODEL), BF16)],
        compiler_params=pltpu.CompilerParams(
            dimension_semantics=("parallel",), vmem_limit_bytes=VMEM_LIMIT),
        name="proj",
    )(x2d, x2d, x2d, w_qkv, w_zuv, w_gl, conv_w8, gate_params)


def _delta_chains(chains):
    c, nh = CHUNK, N_HEADS
    hc = nh * c
    n_ch = len(chains)
    every = range(n_ch)

    ri = lax.broadcasted_iota(jnp.int32, (hc, hc), 0)
    ci = lax.broadcasted_iota(jnp.int32, (hc, hc), 1)
    same64 = (ri >> 6) == (ci >> 6)
    same32 = (ri >> 5) == (ci >> 5)
    same16 = (ri >> 4) == (ci >> 4)
    eye = jnp.where(ri == ci, 1.0, 0.0)
    incl_of = {True: same64 & (ri >= ci), False: same64 & (ri <= ci)}
    strict_of = {d: m & (ri != ci) for d, m in incl_of.items()}

    q, k, v, gc_col, beta, g_last, k_dec, qg, decay, incl = ([] for _ in range(10))
    for qkv, gcol, grow, _, _, fwd in chains:
        off = 0 if fwd else nh

        def stack(base):
            return jnp.concatenate(
                [qkv[:, (base + h) * HEAD_DIM:(base + h + 1) * HEAD_DIM] for h in range(nh)], 0
            ).astype(F32)

        q.append(stack(0))
        k.append(stack(nh))
        v.append(stack(2 * nh))
        col = jnp.concatenate([gcol[:, off + h:off + h + 1] for h in range(nh)], 0)
        row = jnp.concatenate([grow[off + h:off + h + 1, :] for h in range(nh)], 1)
        gc_col.append(col)
        beta.append(jnp.concatenate(
            [gcol[:, 2 * nh + off + h:2 * nh + off + h + 1] for h in range(nh)], 0))
        edge = c - 1 if fwd else 0
        gl = [gcol[edge:edge + 1, off + h:off + h + 1] for h in range(nh)]
        g_last.append(gl)
        gl_col = jnp.concatenate([jnp.broadcast_to(g, (c, 1)) for g in gl], 0)
        k_dec.append(k[-1] * jnp.exp(gl_col - col))
        qg.append(q[-1] * jnp.exp(col))
        m = incl_of[fwd]
        incl.append(m)
        decay.append(jnp.where(m, jnp.exp(jnp.where(m, col - row, 0.0)), 0.0))

    kb = [k[i] * beta[i] for i in every]
    a = [jnp.where(strict_of[chains[i][5]], _bdot_nt(kb[i], k[i]) * decay[i], 0.0) for i in every]
    qk = [jnp.where(incl[i], _bdot_nt(q[i], k[i]) * decay[i], 0.0) for i in every]

    p = [jnp.where(same16, a[i], 0.0) for i in every]
    t = [eye - p[i] for i in every]
    for _ in range(3):
        p = [_bdot(p[i], p[i]) for i in every]
        t = [t[i] + _bdot(t[i], p[i]) for i in every]
    m = [_bdot(jnp.where(same32, jnp.where(same16, 0.0, a[i]), 0.0), t[i]) for i in every]
    t = [t[i] - _bdot(t[i], m[i]) for i in every]
    m = [_bdot(jnp.where(same32, 0.0, a[i]), t[i]) for i in every]
    t = [t[i] - _bdot(t[i], m[i]) for i in every]

    uw = [_bdot(t[i], jnp.concatenate([v[i] * beta[i], kb[i] * jnp.exp(gc_col[i])], 1))
          for i in every]

    ws = [[_bdot(jnp.concatenate([uw[i][h * c:(h + 1) * c, HEAD_DIM:],
                                  qg[i][h * c:(h + 1) * c]], 0), chains[i][3][h])
           for h in range(nh)] for i in every]
    v_new = [[uw[i][h * c:(h + 1) * c, :HEAD_DIM] - ws[i][h][:c] for h in range(nh)]
             for i in every]
    o = [jnp.concatenate([ws[i][h][c:] for h in range(nh)], 0)
         + _bdot(qk[i], jnp.concatenate(v_new[i], 0)) for i in every]
    for i in every:
        s_ref, o_ref = chains[i][3], chains[i][4]
        for h in range(nh):
            rows = slice(h * c, (h + 1) * c)
            s_ref[h] = (s_ref[h] * jnp.exp(g_last[i][h])
                        + _bdot_tn(k_dec[i][rows], v_new[i][h]))
            o_ref[:, h * HEAD_DIM:(h + 1) * HEAD_DIM] = o[i][rows]


def _delta_kernel(qkv_f_ref, gcol_f_ref, grow_f_ref, qkv_b_ref, gcol_b_ref, grow_b_ref,
                  of_ref, ob_ref, sf_ref, sb_ref):
    @pl.when(pl.program_id(1) == 0)
    def _():
        sf_ref[...] = jnp.zeros_like(sf_ref)
        sb_ref[...] = jnp.zeros_like(sb_ref)

    chains = []
    for bb in range(qkv_f_ref.shape[0]):
        chains.append((qkv_f_ref[bb], gcol_f_ref[bb], grow_f_ref[bb, 0], sf_ref.at[bb],
                       of_ref.at[bb], True))
        chains.append((qkv_b_ref[bb], gcol_b_ref[bb], grow_b_ref[bb, 0], sb_ref.at[bb],
                       ob_ref.at[bb], False))
    _delta_chains(chains)


def _delta(qkv, gcol, grow, batch, seq, bb):
    nc = seq // CHUNK
    fwd = lambda b, j: (b, j, 0)
    bwd = lambda b, j: (b, nc - 1 - j, 0)
    fwd4 = lambda b, j: (b, j, 0, 0)
    bwd4 = lambda b, j: (b, nc - 1 - j, 0, 0)
    return pl.pallas_call(
        _delta_kernel,
        grid=(batch // bb, nc),
        in_specs=[
            pl.BlockSpec((bb, CHUNK, 3 * D_DELTA), fwd),
            pl.BlockSpec((bb, CHUNK, 4 * N_HEADS), fwd),
            pl.BlockSpec((bb, 1, 4 * N_HEADS, CHUNK), fwd4),
            pl.BlockSpec((bb, CHUNK, 3 * D_DELTA), bwd),
            pl.BlockSpec((bb, CHUNK, 4 * N_HEADS), bwd),
            pl.BlockSpec((bb, 1, 4 * N_HEADS, CHUNK), bwd4),
        ],
        out_specs=[
            pl.BlockSpec((bb, CHUNK, D_DELTA), fwd),
            pl.BlockSpec((bb, CHUNK, D_DELTA), bwd),
        ],
        out_shape=[
            jax.ShapeDtypeStruct((batch, seq, D_DELTA), F32),
            jax.ShapeDtypeStruct((batch, seq, D_DELTA), F32),
        ],
        scratch_shapes=[
            pltpu.VMEM((bb, N_HEADS, HEAD_DIM, HEAD_DIM), F32),
            pltpu.VMEM((bb, N_HEADS, HEAD_DIM, HEAD_DIM), F32),
        ],
        compiler_params=pltpu.CompilerParams(
            dimension_semantics=("arbitrary", "arbitrary"), vmem_limit_bytes=VMEM_LIMIT),
        name="delta",
    )(qkv, gcol, grow, qkv, gcol, grow)


def _gelu(x):
    return 0.5 * x * (1.0 + lax.erf(x * (2.0 ** -0.5)))


def _split_bf16(x):
    hi = x.astype(BF16)
    lo = (x - hi.astype(F32)).astype(BF16)
    return hi, lo


def _mix_kernel(of_ref, ob_ref, z_ref, u_ref, v_ref, x_ref, nw_ref, sg_ref, sb_ref, wsp_ref,
                bsp_ref, wout_ref, lg_ref, lb_ref, wr_ref, rb_ref,
                x1_ref, meta_ref, cnt_ref, *, alpha):
    tm = x_ref.shape[0]

    @pl.when(pl.program_id(0) == 0)
    def _():
        cnt_ref[...] = jnp.zeros_like(cnt_ref)

    o = of_ref[...] + ob_ref[...]
    z = z_ref[...].astype(F32)
    nw = nw_ref[...]
    parts = []
    for h in range(N_HEADS):
        sl = slice(h * HEAD_DIM, (h + 1) * HEAD_DIM)
        seg = o[:, sl]
        seg = seg * lax.rsqrt(jnp.mean(seg * seg, -1, keepdims=True) + RMS_EPS) * nw
        zs = z[:, sl]
        parts.append((seg * (zs * jax.nn.sigmoid(zs))).astype(BF16))

    u = _gelu(u_ref[...].astype(F32))
    v = _layer_norm(_gelu(v_ref[...].astype(F32)), sg_ref[...], sb_ref[...]).astype(BF16)
    for g in range(N_GROUPS_SGU):
        sl = slice(g * HEAD_DIM, (g + 1) * HEAD_DIM)
        w_g = wsp_ref[g]
        blocks = []
        for ch in range(tm // SGU_CHUNK):
            rows = slice(ch * SGU_CHUNK, (ch + 1) * SGU_CHUNK)
            mixed = jnp.dot(w_g, v[rows, sl], preferred_element_type=F32) + bsp_ref[g]
            blocks.append(u[rows, sl] * mixed)
        parts.append(jnp.concatenate(blocks, 0).astype(BF16))

    mix = jnp.dot(jnp.concatenate(parts, -1), wout_ref[...], preferred_element_type=F32)
    x1 = _layer_norm(alpha * x_ref[...] + mix, lg_ref[...], lb_ref[...])
    _store_slabs(x1_ref, x1)

    x_hi, x_lo = _split_bf16(x1)
    w_hi, w_lo = _split_bf16(wr_ref[...])
    nt = (((1,), (1,)), ((), ()))
    logits = (lax.dot_general(w_hi, x_hi, nt, preferred_element_type=F32)
              + lax.dot_general(w_lo, x_hi, nt, preferred_element_type=F32)
              + lax.dot_general(w_hi, x_lo, nt, preferred_element_type=F32))
    scores = jax.nn.sigmoid(logits)
    sel = scores + rb_ref[:, 0:1]

    def row(m, e):
        return m[e:e + 1, :]

    best = jnp.zeros((1, tm), jnp.int32)
    best_score = None
    for gi in range(N_EXPERT_GROUPS):
        vals = [row(sel, gi * EXPERTS_PER_GROUP + j) for j in range(EXPERTS_PER_GROUP)]
        top2 = None
        for i in range(EXPERTS_PER_GROUP):
            for j in range(i + 1, EXPERTS_PER_GROUP):
                pair = vals[i] + vals[j]
                top2 = pair if top2 is None else jnp.maximum(top2, pair)
        if gi == 0:
            best_score = top2
        else:
            better = top2 > best_score
            best = jnp.where(better, gi, best)
            best_score = jnp.where(better, top2, best_score)

    def pick(m, j):
        out = row(m, j)
        for gi in range(1, N_EXPERT_GROUPS):
            out = jnp.where(best == gi, row(m, gi * EXPERTS_PER_GROUP + j), out)
        return out

    cand = [pick(sel, j) for j in range(EXPERTS_PER_GROUP)]
    raw = [pick(scores, j) for j in range(EXPERTS_PER_GROUP)]
    i1, m1, s1 = jnp.zeros((1, tm), jnp.int32), cand[0], raw[0]
    for j in range(1, EXPERTS_PER_GROUP):
        gt = cand[j] > m1
        i1 = jnp.where(gt, j, i1)
        m1 = jnp.where(gt, cand[j], m1)
        s1 = jnp.where(gt, raw[j], s1)
    i2 = jnp.zeros((1, tm), jnp.int32)
    m2 = jnp.full((1, tm), -jnp.inf, F32)
    s2 = jnp.zeros((1, tm), F32)
    for j in range(EXPERTS_PER_GROUP):
        cj = jnp.where(i1 == j, -jnp.inf, cand[j])
        gt = cj > m2
        i2 = jnp.where(gt, j, i2)
        m2 = jnp.where(gt, cj, m2)
        s2 = jnp.where(gt, raw[j], s2)
    e1 = best * EXPERTS_PER_GROUP + i1
    e2 = best * EXPERTS_PER_GROUP + i2
    denom = s1 + s2
    g1 = s1 / denom
    g2 = s2 / denom

    eidx = lax.broadcasted_iota(jnp.int32, (N_EXPERTS, tm), 0)
    hit1 = eidx == e1
    hit2 = eidx == e2
    cnt = jnp.where(hit1, 1.0, jnp.where(hit2, 1.0, 0.0))
    tr = lax.broadcasted_iota(jnp.int32, (tm, tm), 0)
    tc = lax.broadcasted_iota(jnp.int32, (tm, tm), 1)
    before = jnp.where(tr < tc, 1.0, 0.0).astype(BF16)
    ranks = jnp.dot(cnt.astype(BF16), before, preferred_element_type=F32) + cnt_ref[:, 0:1]
    r1 = jnp.sum(jnp.where(hit1, ranks, 0.0), 0, keepdims=True)
    r2 = jnp.sum(jnp.where(hit2, ranks, 0.0), 0, keepdims=True)
    cnt_ref[...] = cnt_ref[...] + jnp.sum(cnt, 1, keepdims=True)
    meta_ref[...] = jnp.concatenate(
        [e1.astype(F32), e2.astype(F32), r1, r2, g1, g2, jnp.zeros((2, tm), F32)], 0)


def _mix(o_f, o_b, zuv, x2d, norm_w, sgu_g, sgu_b, w_sp, b_sp, w_out, ln_g, ln_b,
         w_router_t, router_bias, alpha, tm):
    n = x2d.shape[0]
    row = lambda i: (i, 0)
    const2 = lambda i: (0, 0)
    const3 = lambda i: (0, 0, 0)
    return pl.pallas_call(
        functools.partial(_mix_kernel, alpha=alpha),
        grid=(n // tm,),
        in_specs=[
            pl.BlockSpec((tm, D_DELTA), row),
            pl.BlockSpec((tm, D_DELTA), row),
            pl.BlockSpec((tm, D_DELTA), lambda i: (i, 0)),
            pl.BlockSpec((tm, D_SGU), lambda i: (i, 1)),
            pl.BlockSpec((tm, D_SGU), lambda i: (i, 2)),
            pl.BlockSpec((tm, D_MODEL), row),
            pl.BlockSpec((1, HEAD_DIM), const2),
            pl.BlockSpec((1, D_SGU), const2),
            pl.BlockSpec((1, D_SGU), const2),
            pl.BlockSpec((N_GROUPS_SGU, SGU_CHUNK, SGU_CHUNK), const3),
            pl.BlockSpec((N_GROUPS_SGU, SGU_CHUNK, HEAD_DIM), const3),
            pl.BlockSpec((D_MODEL, D_MODEL), const2),
            pl.BlockSpec((1, D_MODEL), const2),
            pl.BlockSpec((1, D_MODEL), const2),
            pl.BlockSpec((N_EXPERTS, D_MODEL), const2),
            pl.BlockSpec((N_EXPERTS, 128), const2),
        ],
        out_specs=[
            pl.BlockSpec((tm * SLAB, LANES), row),
            pl.BlockSpec((8, tm), lambda i: (0, i)),
            pl.BlockSpec((N_EXPERTS, 128), const2),
        ],
        out_shape=[
            jax.ShapeDtypeStruct((n * SLAB, LANES), F32),
            jax.ShapeDtypeStruct((8, n), F32),
            jax.ShapeDtypeStruct((N_EXPERTS, 128), F32),
        ],
        compiler_params=pltpu.CompilerParams(
            dimension_semantics=("arbitrary",), vmem_limit_bytes=VMEM_LIMIT),
        name="mix",
    )(o_f, o_b, zuv, zuv, zuv, x2d, norm_w, sgu_g, sgu_b, w_sp, b_sp, w_out, ln_g, ln_b,
      w_router_t, router_bias)


def _dispatch_kernel(p1_ref, p2_ref, zt_ref, x_ref, xs_hbm, zero_buf, sem, zsem):
    i = pl.program_id(0)
    td = x_ref.shape[0] // SLAB
    zrows = zero_buf.shape[0]

    @pl.when(i == 0)
    def _():
        zero_buf[...] = jnp.zeros_like(zero_buf)
        copies = [
            pltpu.make_async_copy(
                zero_buf,
                xs_hbm.at[pl.ds(pl.multiple_of(jnp.maximum(zt_ref[j], 0) * zrows, zrows), zrows), :],
                zsem)
            for j in range(zt_ref.shape[0])]
        for j, c in enumerate(copies):
            pl.when(zt_ref[j] >= 0)(c.start)
        for j, c in enumerate(copies):
            pl.when(zt_ref[j] >= 0)(c.wait)

    base = i * td
    for r in range(td):
        src = x_ref.at[pl.ds(r * SLAB, SLAB), :]
        for p_ref in (p1_ref, p2_ref):
            row = pl.multiple_of(p_ref[base + r] * SLAB, SLAB)
            pltpu.make_async_copy(src, xs_hbm.at[pl.ds(row, SLAB), :], sem).start()
    for _ in range(2):
        pltpu.make_async_copy(x_ref, xs_hbm.at[pl.ds(0, td * SLAB), :], sem).wait()


def _dispatch(pos1, pos2, zero_tiles, x1s, total_rows, tm_e, td):
    n = x1s.shape[0] // SLAB
    grid_spec = pltpu.PrefetchScalarGridSpec(
        num_scalar_prefetch=3,
        grid=(n // td,),
        in_specs=[pl.BlockSpec((td * SLAB, LANES), lambda i, p1, p2, zt: (i, 0))],
        out_specs=pl.BlockSpec(memory_space=pl.ANY),
        scratch_shapes=[pltpu.VMEM((tm_e * SLAB, LANES), F32), pltpu.SemaphoreType.DMA(()),
                        pltpu.SemaphoreType.DMA(())],
    )
    return pl.pallas_call(
        _dispatch_kernel,
        grid_spec=grid_spec,
        out_shape=jax.ShapeDtypeStruct((total_rows * SLAB, LANES), F32),
        compiler_params=pltpu.CompilerParams(
            dimension_semantics=("arbitrary",), vmem_limit_bytes=VMEM_LIMIT),
        name="dispatch",
    )(pos1, pos2, zero_tiles, x1s)


def _expert_kernel(te_ref, nt_ref, xs_ref, wg_ref, wu_ref, wd_ref, y_ref, wgb, wub, wdb):
    i = pl.program_id(0)
    tm = xs_ref.shape[0] // SLAB
    ntiles = nt_ref[0]

    @pl.when(i < ntiles)
    def _():
        @pl.when((i == 0) | (te_ref[i] != te_ref[jnp.maximum(i - 1, 0)]))
        def _():
            wgb[...] = wg_ref[...].astype(BF16)
            wub[...] = wu_ref[...].astype(BF16)
            wdb[...] = wd_ref[...].astype(BF16)

        xs = jnp.concatenate(_load_slabs(xs_ref, tm), -1).astype(BF16)
        hg = jnp.dot(xs, wgb[...], preferred_element_type=F32)
        hu = jnp.dot(xs, wub[...], preferred_element_type=F32)
        hid = (hg * jax.nn.sigmoid(hg) * hu).astype(BF16)
        _store_slabs(y_ref, jnp.dot(hid, wdb[...], preferred_element_type=F32))

    @pl.when(i >= ntiles)
    def _():
        y_ref[...] = jnp.zeros_like(y_ref)


def _experts(xs, tile_expert, ntiles, w_gate, w_up, w_down, layer, tm):
    max_tiles = tile_expert.shape[0]

    def wmap(i, te, nt):
        return (layer, te[jnp.minimum(i, nt[0] - 1)], 0, 0)

    wspec = pl.BlockSpec((None, None, D_MODEL, D_MODEL), wmap)
    grid_spec = pltpu.PrefetchScalarGridSpec(
        num_scalar_prefetch=2,
        grid=(max_tiles,),
        in_specs=[pl.BlockSpec((tm * SLAB, LANES), lambda i, te, nt: (jnp.minimum(i, nt[0] - 1), 0)),
                  wspec, wspec, wspec],
        out_specs=pl.BlockSpec((tm * SLAB, LANES), lambda i, te, nt: (i, 0)),
        scratch_shapes=[pltpu.VMEM((D_MODEL, D_MODEL), BF16)] * 3,
    )
    return pl.pallas_call(
        _expert_kernel,
        grid_spec=grid_spec,
        out_shape=jax.ShapeDtypeStruct(xs.shape, F32),
        compiler_params=pltpu.CompilerParams(
            dimension_semantics=("arbitrary",), vmem_limit_bytes=VMEM_LIMIT),
        name="experts",
    )(tile_expert, ntiles, xs, w_gate, w_up, w_down)


def _combine_kernel(p1_ref, p2_ref, y_hbm, x1_ref, gate_ref, lg_ref, lb_ref, o_ref, buf, sem,
                    *, alpha):
    i = pl.program_id(0)
    n_steps = pl.num_programs(0)
    tm = o_ref.shape[0]

    def gather(tile, slot):
        base = tile * tm
        for r in range(tm):
            for k, p_ref in enumerate((p1_ref, p2_ref)):
                row = pl.multiple_of(p_ref[base + r] * SLAB, SLAB)
                pltpu.make_async_copy(y_hbm.at[pl.ds(row, SLAB), :],
                                      buf.at[slot, k, pl.ds(r * SLAB, SLAB), :],
                                      sem.at[slot]).start()

    def wait(slot):
        for k in range(2):
            pltpu.make_async_copy(y_hbm.at[pl.ds(0, tm * SLAB), :], buf.at[slot, k],
                                  sem.at[slot]).wait()

    @pl.when(i == 0)
    def _():
        gather(0, 0)

    def step(slot):
        wait(slot)
        gather(jnp.minimum(i + 1, n_steps - 1), 1 - slot)
        gates = gate_ref[...]
        g1, g2 = gates[:, 4:5], gates[:, 5:6]
        x1 = _load_slabs(x1_ref, tm)
        y1 = _load_slabs(buf.at[slot, 0], tm)
        y2 = _load_slabs(buf.at[slot, 1], tm)
        y = jnp.concatenate([alpha * x1[s] + g1 * y1[s] + g2 * y2[s] for s in range(SLAB)], -1)
        o_ref[...] = _layer_norm(y, lg_ref[...], lb_ref[...])

        @pl.when(i == n_steps - 1)
        def _():
            wait(1 - slot)

    for parity in range(2):
        pl.when(i % 2 == parity)(functools.partial(step, parity))


def _combine(pos1, pos2, ys, x1, gates, ln_g, ln_b, alpha, tm):
    n = x1.shape[0] // SLAB
    grid_spec = pltpu.PrefetchScalarGridSpec(
        num_scalar_prefetch=2,
        grid=(n // tm,),
        in_specs=[
            pl.BlockSpec(memory_space=pl.ANY),
            pl.BlockSpec((tm * SLAB, LANES), lambda i, p1, p2: (i, 0)),
            pl.BlockSpec((tm, 8), lambda i, p1, p2: (i, 0)),
            pl.BlockSpec((1, D_MODEL), lambda i, p1, p2: (0, 0)),
            pl.BlockSpec((1, D_MODEL), lambda i, p1, p2: (0, 0)),
        ],
        out_specs=pl.BlockSpec((tm, D_MODEL), lambda i, p1, p2: (i, 0)),
        scratch_shapes=[pltpu.VMEM((2, 2, tm * SLAB, LANES), F32), pltpu.SemaphoreType.DMA((2,))],
    )
    return pl.pallas_call(
        functools.partial(_combine_kernel, alpha=alpha),
        grid_spec=grid_spec,
        out_shape=jax.ShapeDtypeStruct((n, D_MODEL), F32),
        compiler_params=pltpu.CompilerParams(
            dimension_semantics=("arbitrary",), vmem_limit_bytes=VMEM_LIMIT),
        name="combine",
    )(pos1, pos2, ys, x1, gates, ln_g, ln_b)


def _routing_tables(meta, counts, tm_e):
    n = meta.shape[1]
    max_tiles = (2 * n) // tm_e + N_EXPERTS
    cnt = counts[:, 0].astype(jnp.int32)
    tiles = (cnt + tm_e - 1) // tm_e
    tile_end = jnp.cumsum(tiles)
    row_start = (tile_end - tiles) * tm_e
    e1 = meta[0].astype(jnp.int32)
    e2 = meta[1].astype(jnp.int32)
    pos1 = row_start[e1] + meta[2].astype(jnp.int32)
    pos2 = row_start[e2] + meta[3].astype(jnp.int32)
    ntiles = tile_end[-1:]
    tile_ids = jnp.arange(max_tiles, dtype=jnp.int32)
    tile_expert = jnp.minimum(
        jnp.sum((tile_ids[:, None] >= tile_end[None, :]).astype(jnp.int32), 1), N_EXPERTS - 1)
    tail = ntiles[0] + jnp.arange(N_EXPERTS, dtype=jnp.int32)
    zero_tiles = jnp.concatenate([
        jnp.where(tiles > 0, tile_end - 1, -1), jnp.where(tail < max_tiles, tail, -1)])
    return pos1, pos2, tile_expert, ntiles.astype(jnp.int32), zero_tiles.astype(jnp.int32)


def _forward(x, w_in, conv_w, a_log_f, dt_bias_f, a_log_b, dt_bias_b, delta_norm_w,
             sgu_ln_g, sgu_ln_b, w_spatial, b_spatial, w_out, ln1_g, ln1_b,
             w_router, router_bias, w_gate, w_up, w_down, ln2_g, ln2_b,
             *, tm_proj, bb_delta, tm_mix, td_disp, tm_exp, tm_comb):
    batch, seq, d = x.shape
    depth = w_in.shape[0]
    n = batch * seq
    alpha = (2.0 * depth) ** 0.25
    x2d = x.reshape(n, d)

    g0 = 4 * D_DELTA
    w_qkv = w_in[:, :, :3 * D_DELTA].astype(BF16)
    w_zuv = jnp.concatenate([w_in[:, :, 3 * D_DELTA:g0], w_in[:, :, g0 + 16:]], -1).astype(BF16)
    w_gl = jnp.pad(w_in[:, :, g0:g0 + 16], ((0, 0), (0, 0), (0, GATE_COLS - 16))).astype(BF16)
    conv_w8 = jnp.pad(conv_w, ((0, 0), (0, 8 - CONV_WIDTH), (0, 0)))
    pad_lanes = lambda a, b: jnp.pad(jnp.concatenate([a, b], -1), ((0, 0), (0, GATE_COLS - 8)))
    gate_params = jnp.pad(
        jnp.stack([pad_lanes(a_log_f, a_log_b), pad_lanes(dt_bias_f, dt_bias_b)], 1),
        ((0, 0), (0, 6), (0, 0)))
    w_sp = w_spatial.astype(BF16)
    b_sp = jnp.broadcast_to(b_spatial[..., None], b_spatial.shape + (HEAD_DIM,))
    w_out_b = w_out.astype(BF16)
    w_router_t = w_router.T
    rb = jnp.broadcast_to(router_bias[:, None], (N_EXPERTS, 128))
    assert (2 * n) % tm_exp == 0
    max_tiles = (2 * n) // tm_exp + N_EXPERTS

    for l in range(depth):
        qkv, zuv, gcol = _proj(x2d, w_qkv[l], w_zuv[l], w_gl[l], conv_w8[l], gate_params[l], seq,
                               tm_proj)
        gcol3 = gcol.reshape(batch, seq, 4 * N_HEADS)
        grow = gcol.reshape(batch, seq // CHUNK, CHUNK, 4 * N_HEADS).transpose(0, 1, 3, 2)
        o_f, o_b = _delta(qkv.reshape(batch, seq, 3 * D_DELTA), gcol3, grow, batch, seq, bb_delta)
        o_f, o_b = o_f.reshape(n, D_DELTA), o_b.reshape(n, D_DELTA)
        x1, meta, counts = _mix(
            o_f, o_b, zuv, x2d, delta_norm_w[l][None], sgu_ln_g[l][None], sgu_ln_b[l][None],
            w_sp[l], b_sp[l], w_out_b[l], ln1_g[l][None], ln1_b[l][None], w_router_t, rb,
            alpha, tm_mix)
        pos1, pos2, tile_expert, ntiles, zero_tiles = _routing_tables(meta, counts, tm_exp)
        xs = _dispatch(pos1, pos2, zero_tiles, x1, max_tiles * tm_exp, tm_exp, td_disp)
        ys = _experts(xs, tile_expert, ntiles, w_gate, w_up, w_down, l, tm_exp)
        x2d = _combine(pos1, pos2, ys, x1, meta.T, ln2_g[l][None], ln2_b[l][None], alpha, tm_comb)
    return x2d.reshape(batch, seq, d)


def kernel(x, w_in, conv_w, a_log_f, dt_bias_f, a_log_b, dt_bias_b, delta_norm_w, sgu_ln_g,
           sgu_ln_b, w_spatial, b_spatial, w_out, ln1_g, ln1_b, w_router, router_bias, w_gate,
           w_up, w_down, ln2_g, ln2_b):
    return _forward(x, w_in, conv_w, a_log_f, dt_bias_f, a_log_b, dt_bias_b, delta_norm_w,
                    sgu_ln_g, sgu_ln_b, w_spatial, b_spatial, w_out, ln1_g, ln1_b, w_router,
                    router_bias, w_gate, w_up, w_down, ln2_g, ln2_b,
                    tm_proj=512, bb_delta=4, tm_mix=512, td_disp=256, tm_exp=512,
                    tm_comb=256)
```

```python
import functools

import jax
import jax.numpy as jnp
from jax import lax
from jax.experimental import pallas as pl
from jax.experimental.pallas import tpu as pltpu

F32 = jnp.float32
BF16 = jnp.bfloat16

D_MODEL = 1024
D_DELTA = 512
D_SGU = 512
HEAD_DIM = 128
N_HEADS = 4
CONV_WIDTH = 5
CHUNK = 64
SGU_CHUNK = 128
N_GROUPS_SGU = 4
N_EXPERTS = 16
N_EXPERT_GROUPS = 4
EXPERTS_PER_GROUP = 4
LN_EPS = 1e-5
RMS_EPS = 1e-6
L2_EPS = 1e-6

GATE_COLS = 128
HALO_ROWS = 16
LANES = 128
SLAB = D_MODEL // LANES
VMEM_LIMIT = 56 * 1024 * 1024


def _bdot(a, b):
    return jnp.dot(a.astype(BF16), b.astype(BF16), preferred_element_type=F32)


def _bdot_nt(a, b):
    return lax.dot_general(a.astype(BF16), b.astype(BF16), (((1,), (1,)), ((), ())),
                           preferred_element_type=F32)


def _bdot_tn(a, b):
    return lax.dot_general(a.astype(BF16), b.astype(BF16), (((0,), (0,)), ((), ())),
                           preferred_element_type=F32)


def _layer_norm(y, g, b):
    mu = jnp.mean(y, -1, keepdims=True)
    d = y - mu
    var = jnp.mean(d * d, -1, keepdims=True)
    return d * lax.rsqrt(var + LN_EPS) * g + b


def _load_slabs(ref, rows):
    return [ref[pl.ds(s, rows, stride=SLAB), :] for s in range(SLAB)]


def _store_slabs(ref, mat):
    rows = mat.shape[0]
    for s in range(SLAB):
        ref[pl.ds(s, rows, stride=SLAB), :] = mat[:, s * LANES:(s + 1) * LANES]


def _conv_taps(ext, w, lo, n):
    rows = ext.shape[0]
    acc = None
    for j in range(CONV_WIDTH):
        shift = (-(j - CONV_WIDTH // 2)) % rows
        r = ext if shift == 0 else pltpu.roll(ext, shift, 0)
        term = r[lo:lo + n] * w[j:j + 1]
        acc = term if acc is None else acc + term
    return acc


def _conv_post(y, first_head):
    s = y * jax.nn.sigmoid(y)
    outs = []
    for j in range(y.shape[1] // HEAD_DIM):
        hh = first_head + j
        seg = s[:, j * HEAD_DIM:(j + 1) * HEAD_DIM]
        if hh < 2 * N_HEADS:
            ss = jnp.sum(seg * seg, -1, keepdims=True)
            seg = seg * lax.rsqrt(ss + L2_EPS)
            if hh < N_HEADS:
                seg = seg * (HEAD_DIM ** -0.5)
        outs.append(seg)
    return jnp.concatenate(outs, -1).astype(BF16)


def _proj_kernel(cur_ref, prev_ref, next_ref, wqkv_ref, wzuv_ref, wgl_ref, cw_ref, gp_ref,
                 qkv_ref, zuv_ref, gcol_ref, xe_ref, *, tiles_per_seq, col_chunk):
    i = pl.program_id(0)
    tb = cur_ref.shape[0]
    h = HALO_ROWS
    pos = i % tiles_per_seq
    xc = cur_ref[...].astype(BF16)
    xp = jnp.where(pos == 0, 0.0, prev_ref[...]).astype(BF16)
    xn = jnp.where(pos == tiles_per_seq - 1, 0.0, next_ref[...]).astype(BF16)
    xe_ref[0:h, :] = xp
    xe_ref[h:h + tb, :] = xc
    xe_ref[h + tb:, :] = xn
    _proj_body(xe_ref, wqkv_ref, wzuv_ref, wgl_ref, cw_ref, gp_ref, qkv_ref, zuv_ref, gcol_ref,
               col_chunk)


def _proj_body(xe_ref, wqkv_ref, wzuv_ref, wgl_ref, cw_ref, gp_ref, qkv_ref, zuv_ref, gcol_ref,
               col_chunk):
    h = HALO_ROWS
    tb = xe_ref.shape[0] - 2 * h
    n_chunks = 3 * D_DELTA // col_chunk
    cols = [slice(c * col_chunk, (c + 1) * col_chunk) for c in range(n_chunks)]

    def conv(c, h_ext):
        y = _conv_taps(h_ext, cw_ref[:, cols[c]], h, tb)
        qkv_ref[:, cols[c]] = _conv_post(y, c * col_chunk // HEAD_DIM)

    h_prev = jnp.dot(xe_ref[...], wqkv_ref[:, cols[0]], preferred_element_type=F32)
    for c in range(1, n_chunks):
        h_cur = jnp.dot(xe_ref[...], wqkv_ref[:, cols[c]], preferred_element_type=F32)
        conv(c - 1, h_prev)
        h_prev = h_cur
    for c in range(n_chunks):
        zuv_ref[:, cols[c]] = jnp.dot(xe_ref[h:h + tb, :], wzuv_ref[:, cols[c]],
                                      preferred_element_type=F32).astype(BF16)
        if c == 0:
            conv(n_chunks - 1, h_prev)

    gl = jnp.dot(xe_ref[h:h + tb, :], wgl_ref[...], preferred_element_type=F32)
    neg_a = -jnp.exp(gp_ref[0:1, :])
    xg = gl + gp_ref[1:2, :]
    softplus = jnp.maximum(xg, 0.0) + jnp.log1p(jnp.exp(-jnp.abs(xg)))
    g = neg_a * softplus
    beta = jax.nn.sigmoid(gl)
    lane = lax.broadcasted_iota(jnp.int32, (CHUNK, GATE_COLS), 1)
    r = lax.broadcasted_iota(jnp.int32, (2 * CHUNK, CHUNK), 0)
    c = lax.broadcasted_iota(jnp.int32, (2 * CHUNK, CHUNK), 1)
    tri = jnp.where(r < CHUNK, jnp.where(r >= c, 1.0, 0.0), jnp.where(r - CHUNK <= c, 1.0, 0.0))
    for ch in range(tb // CHUNK):
        sl = slice(ch * CHUNK, (ch + 1) * CHUNK)
        cs = jnp.dot(tri, g[sl], precision=lax.Precision.HIGHEST, preferred_element_type=F32)
        out = jnp.where(lane < N_HEADS, cs[:CHUNK],
                        jnp.where(lane < 2 * N_HEADS, cs[CHUNK:], beta[sl]))
        gcol_ref[sl, :] = out[:, :4 * N_HEADS]


def _proj(x2d, w_qkv, w_zuv, w_gl, conv_w8, gate_params, seq, tb):
    n = x2d.shape[0]
    hb = tb // HALO_ROWS
    last_hblk = n // HALO_ROWS - 1
    const = lambda i: (0, 0)
    return pl.pallas_call(
        functools.partial(_proj_kernel, tiles_per_seq=seq // tb, col_chunk=4 * HEAD_DIM),
        grid=(n // tb,),
        in_specs=[
            pl.BlockSpec((tb, D_MODEL), lambda i: (i, 0)),
            pl.BlockSpec((HALO_ROWS, D_MODEL), lambda i: (jnp.maximum(i * hb - 1, 0), 0)),
            pl.BlockSpec((HALO_ROWS, D_MODEL), lambda i: (jnp.minimum((i + 1) * hb, last_hblk), 0)),
            pl.BlockSpec((D_MODEL, 3 * D_DELTA), const),
            pl.BlockSpec((D_MODEL, 3 * D_DELTA), const),
            pl.BlockSpec((D_MODEL, GATE_COLS), const),
            pl.BlockSpec((8, 3 * D_DELTA), const),
            pl.BlockSpec((8, GATE_COLS), const),
        ],
        out_specs=[
            pl.BlockSpec((tb, 3 * D_DELTA), lambda i: (i, 0)),
            pl.BlockSpec((tb, 3 * D_DELTA), lambda i: (i, 0)),
            pl.BlockSpec((tb, 4 * N_HEADS), lambda i: (i, 0)),
        ],
        out_shape=[
            jax.ShapeDtypeStruct((n, 3 * D_DELTA), BF16),
            jax.ShapeDtypeStruct((n, 3 * D_DELTA), BF16),
            jax.ShapeDtypeStruct((n, 4 * N_HEADS), F32),
        ],
        scratch_shapes=[pltpu.VMEM((tb + 2 * HALO_ROWS, D_MODEL), BF16)],
        compiler_params=pltpu.CompilerParams(
            dimension_semantics=("parallel",), vmem_limit_bytes=VMEM_LIMIT),
        name="proj",
    )(x2d, x2d, x2d, w_qkv, w_zuv, w_gl, conv_w8, gate_params)


def _delta_chains(chains):
    c, nh = CHUNK, N_HEADS
    hc = nh * c
    n_ch = len(chains)
    every = range(n_ch)

    ri = lax.broadcasted_iota(jnp.int32, (hc, hc), 0)
    ci = lax.broadcasted_iota(jnp.int32, (hc, hc), 1)
    same64 = (ri >> 6) == (ci >> 6)
    same32 = (ri >> 5) == (ci >> 5)
    same16 = (ri >> 4) == (ci >> 4)
    eye = jnp.where(ri == ci, 1.0, 0.0)
    incl_of = {True: same64 & (ri >= ci), False: same64 & (ri <= ci)}
    strict_of = {d: m & (ri != ci) for d, m in incl_of.items()}

    q, k, v, gc_col, beta, g_last, k_dec, qg, decay, incl = ([] for _ in range(10))
    for qkv, gcol, grow, _, _, fwd in chains:
        off = 0 if fwd else nh

        def stack(base):
            return jnp.concatenate(
                [qkv[:, (base + h) * HEAD_DIM:(base + h + 1) * HEAD_DIM] for h in range(nh)], 0
            ).astype(F32)

        q.append(stack(0))
        k.append(stack(nh))
        v.append(stack(2 * nh))
        col = jnp.concatenate([gcol[:, off + h:off + h + 1] for h in range(nh)], 0)
        row = jnp.concatenate([grow[off + h:off + h + 1, :] for h in range(nh)], 1)
        gc_col.append(col)
        beta.append(jnp.concatenate(
            [gcol[:, 2 * nh + off + h:2 * nh + off + h + 1] for h in range(nh)], 0))
        edge = c - 1 if fwd else 0
        gl = [gcol[edge:edge + 1, off + h:off + h + 1] for h in range(nh)]
        g_last.append(gl)
        gl_col = jnp.concatenate([jnp.broadcast_to(g, (c, 1)) for g in gl], 0)
        k_dec.append(k[-1] * jnp.exp(gl_col - col))
        qg.append(q[-1] * jnp.exp(col))
        m = incl_of[fwd]
        incl.append(m)
        decay.append(jnp.where(m, jnp.exp(jnp.where(m, col - row, 0.0)), 0.0))

    kb = [k[i] * beta[i] for i in every]
    a = [jnp.where(strict_of[chains[i][5]], _bdot_nt(kb[i], k[i]) * decay[i], 0.0) for i in every]
    qk = [jnp.where(incl[i], _bdot_nt(q[i], k[i]) * decay[i], 0.0).astype(BF16) for i in every]

    p = [jnp.where(same16, a[i], 0.0) for i in every]
    t = [eye - p[i] for i in every]
    p = [x.astype(BF16) for x in p]
    tb = [x.astype(BF16) for x in t]
    for _ in range(3):
        p = [_bdot(p[i], p[i]).astype(BF16) for i in every]
        t = [t[i] + _bdot(tb[i], p[i]) for i in every]
        tb = [x.astype(BF16) for x in t]
    a32 = [jnp.where(same32, jnp.where(same16, 0.0, a[i]), 0.0).astype(BF16) for i in every]
    a64 = [jnp.where(same32, 0.0, a[i]).astype(BF16) for i in every]
    for a_off in (a32, a64):
        m = [_bdot(a_off[i], tb[i]).astype(BF16) for i in every]
        t = [t[i] - _bdot(tb[i], m[i]) for i in every]
        tb = [x.astype(BF16) for x in t]

    uw = [_bdot(tb[i], jnp.concatenate([v[i] * beta[i], kb[i] * jnp.exp(gc_col[i])], 1))
          for i in every]

    ws = [[_bdot(jnp.concatenate([uw[i][h * c:(h + 1) * c, HEAD_DIM:],
                                  qg[i][h * c:(h + 1) * c]], 0), chains[i][3][h])
           for h in range(nh)] for i in every]
    v_new = [[uw[i][h * c:(h + 1) * c, :HEAD_DIM] - ws[i][h][:c] for h in range(nh)]
             for i in every]
    o = [jnp.concatenate([ws[i][h][c:] for h in range(nh)], 0)
         + _bdot(qk[i], jnp.concatenate(v_new[i], 0)) for i in every]
    for i in every:
        s_ref, o_ref = chains[i][3], chains[i][4]
        for h in range(nh):
            rows = slice(h * c, (h + 1) * c)
            s_ref[h] = (s_ref[h] * jnp.exp(g_last[i][h])
                        + _bdot_tn(k_dec[i][rows], v_new[i][h]))
            o_ref[:, h * HEAD_DIM:(h + 1) * HEAD_DIM] = o[i][rows]


def _delta_kernel(qkv_f_ref, gcol_f_ref, grow_f_ref, qkv_b_ref, gcol_b_ref, grow_b_ref,
                  of_ref, ob_ref, sf_ref, sb_ref):
    @pl.when(pl.program_id(1) == 0)
    def _():
        sf_ref[...] = jnp.zeros_like(sf_ref)
        sb_ref[...] = jnp.zeros_like(sb_ref)

    chains = []
    for bb in range(qkv_f_ref.shape[0]):
        chains.append((qkv_f_ref[bb], gcol_f_ref[bb], grow_f_ref[bb, 0], sf_ref.at[bb],
                       of_ref.at[bb], True))
        chains.append((qkv_b_ref[bb], gcol_b_ref[bb], grow_b_ref[bb, 0], sb_ref.at[bb],
                       ob_ref.at[bb], False))
    _delta_chains(chains)


def _delta(qkv, gcol, grow, batch, seq, bb):
    nc = seq // CHUNK
    fwd = lambda b, j: (b, j, 0)
    bwd = lambda b, j: (b, nc - 1 - j, 0)
    fwd4 = lambda b, j: (b, j, 0, 0)
    bwd4 = lambda b, j: (b, nc - 1 - j, 0, 0)
    return pl.pallas_call(
        _delta_kernel,
        grid=(batch // bb, nc),
        in_specs=[
            pl.BlockSpec((bb, CHUNK, 3 * D_DELTA), fwd),
            pl.BlockSpec((bb, CHUNK, 4 * N_HEADS), fwd),
            pl.BlockSpec((bb, 1, 4 * N_HEADS, CHUNK), fwd4),
            pl.BlockSpec((bb, CHUNK, 3 * D_DELTA), bwd),
            pl.BlockSpec((bb, CHUNK, 4 * N_HEADS), bwd),
            pl.BlockSpec((bb, 1, 4 * N_HEADS, CHUNK), bwd4),
        ],
        out_specs=[
            pl.BlockSpec((bb, CHUNK, D_DELTA), fwd),
            pl.BlockSpec((bb, CHUNK, D_DELTA), bwd),
        ],
        out_shape=[
            jax.ShapeDtypeStruct((batch, seq, D_DELTA), F32),
            jax.ShapeDtypeStruct((batch, seq, D_DELTA), F32),
        ],
        scratch_shapes=[
            pltpu.VMEM((bb, N_HEADS, HEAD_DIM, HEAD_DIM), F32),
            pltpu.VMEM((bb, N_HEADS, HEAD_DIM, HEAD_DIM), F32),
        ],
        compiler_params=pltpu.CompilerParams(
            dimension_semantics=("arbitrary", "arbitrary"), vmem_limit_bytes=VMEM_LIMIT),
        name="delta",
    )(qkv, gcol, grow, qkv, gcol, grow)


def _gelu(x):
    return 0.5 * x * (1.0 + lax.erf(x * (2.0 ** -0.5)))


def _split_bf16(x):
    hi = x.astype(BF16)
    lo = (x - hi.astype(F32)).astype(BF16)
    return hi, lo


def _mix_kernel(of_ref, ob_ref, z_ref, u_ref, v_ref, x_ref, nw_ref, sg_ref, sb_ref, wsp_ref,
                bsp_ref, wout_ref, lg_ref, lb_ref, wr_ref, rb_ref,
                x1_ref, meta_ref, cnt_ref, *, alpha):
    tm = x_ref.shape[0]

    @pl.when(pl.program_id(0) == 0)
    def _():
        cnt_ref[...] = jnp.zeros_like(cnt_ref)

    o = of_ref[...] + ob_ref[...]
    z = z_ref[...].astype(F32)
    nw = nw_ref[...]
    parts = []
    for h in range(N_HEADS):
        sl = slice(h * HEAD_DIM, (h + 1) * HEAD_DIM)
        seg = o[:, sl]
        seg = seg * lax.rsqrt(jnp.mean(seg * seg, -1, keepdims=True) + RMS_EPS) * nw
        zs = z[:, sl]
        parts.append((seg * (zs * jax.nn.sigmoid(zs))).astype(BF16))

    u = _gelu(u_ref[...].astype(F32))
    v = _layer_norm(_gelu(v_ref[...].astype(F32)), sg_ref[...], sb_ref[...]).astype(BF16)
    for g in range(N_GROUPS_SGU):
        sl = slice(g * HEAD_DIM, (g + 1) * HEAD_DIM)
        w_g = wsp_ref[g]
        blocks = []
        for ch in range(tm // SGU_CHUNK):
            rows = slice(ch * SGU_CHUNK, (ch + 1) * SGU_CHUNK)
            mixed = jnp.dot(w_g, v[rows, sl], preferred_element_type=F32) + bsp_ref[g]
            blocks.append(u[rows, sl] * mixed)
        parts.append(jnp.concatenate(blocks, 0).astype(BF16))

    mix = jnp.dot(jnp.concatenate(parts, -1), wout_ref[...], preferred_element_type=F32)
    x1 = _layer_norm(alpha * x_ref[...] + mix, lg_ref[...], lb_ref[...])
    _store_slabs(x1_ref, x1)

    x_hi, x_lo = _split_bf16(x1)
    w_hi, w_lo = _split_bf16(wr_ref[...])
    nt = (((1,), (1,)), ((), ()))
    logits = (lax.dot_general(w_hi, x_hi, nt, preferred_element_type=F32)
              + lax.dot_general(w_lo, x_hi, nt, preferred_element_type=F32)
              + lax.dot_general(w_hi, x_lo, nt, preferred_element_type=F32))
    scores = jax.nn.sigmoid(logits)
    sel = scores + rb_ref[:, 0:1]

    def row(m, e):
        return m[e:e + 1, :]

    best = jnp.zeros((1, tm), jnp.int32)
    best_score = None
    for gi in range(N_EXPERT_GROUPS):
        vals = [row(sel, gi * EXPERTS_PER_GROUP + j) for j in range(EXPERTS_PER_GROUP)]
        top2 = None
        for i in range(EXPERTS_PER_GROUP):
            for j in range(i + 1, EXPERTS_PER_GROUP):
                pair = vals[i] + vals[j]
                top2 = pair if top2 is None else jnp.maximum(top2, pair)
        if gi == 0:
            best_score = top2
        else:
            better = top2 > best_score
            best = jnp.where(better, gi, best)
            best_score = jnp.where(better, top2, best_score)

    def pick(m, j):
        out = row(m, j)
        for gi in range(1, N_EXPERT_GROUPS):
            out = jnp.where(best == gi, row(m, gi * EXPERTS_PER_GROUP + j), out)
        return out

    cand = [pick(sel, j) for j in range(EXPERTS_PER_GROUP)]
    raw = [pick(scores, j) for j in range(EXPERTS_PER_GROUP)]
    i1, m1, s1 = jnp.zeros((1, tm), jnp.int32), cand[0], raw[0]
    for j in range(1, EXPERTS_PER_GROUP):
        gt = cand[j] > m1
        i1 = jnp.where(gt, j, i1)
        m1 = jnp.where(gt, cand[j], m1)
        s1 = jnp.where(gt, raw[j], s1)
    i2 = jnp.zeros((1, tm), jnp.int32)
    m2 = jnp.full((1, tm), -jnp.inf, F32)
    s2 = jnp.zeros((1, tm), F32)
    for j in range(EXPERTS_PER_GROUP):
        cj = jnp.where(i1 == j, -jnp.inf, cand[j])
        gt = cj > m2
        i2 = jnp.where(gt, j, i2)
        m2 = jnp.where(gt, cj, m2)
        s2 = jnp.where(gt, raw[j], s2)
    e1 = best * EXPERTS_PER_GROUP + i1
    e2 = best * EXPERTS_PER_GROUP + i2
    denom = s1 + s2
    g1 = s1 / denom
    g2 = s2 / denom

    eidx = lax.broadcasted_iota(jnp.int32, (N_EXPERTS, tm), 0)
    hit1 = eidx == e1
    hit2 = eidx == e2
    cnt = jnp.where(hit1, 1.0, jnp.where(hit2, 1.0, 0.0))
    tr = lax.broadcasted_iota(jnp.int32, (tm, tm), 0)
    tc = lax.broadcasted_iota(jnp.int32, (tm, tm), 1)
    before = jnp.where(tr < tc, 1.0, 0.0).astype(BF16)
    ranks = jnp.dot(cnt.astype(BF16), before, preferred_element_type=F32) + cnt_ref[:, 0:1]
    r1 = jnp.sum(jnp.where(hit1, ranks, 0.0), 0, keepdims=True)
    r2 = jnp.sum(jnp.where(hit2, ranks, 0.0), 0, keepdims=True)
    cnt_ref[...] = cnt_ref[...] + jnp.sum(cnt, 1, keepdims=True)
    meta_ref[...] = jnp.concatenate(
        [e1.astype(F32), e2.astype(F32), r1, r2, g1, g2, jnp.zeros((2, tm), F32)], 0)


def _mix(o_f, o_b, zuv, x2d, norm_w, sgu_g, sgu_b, w_sp, b_sp, w_out, ln_g, ln_b,
         w_router_t, router_bias, alpha, tm):
    n = x2d.shape[0]
    row = lambda i: (i, 0)
    const2 = lambda i: (0, 0)
    const3 = lambda i: (0, 0, 0)
    return pl.pallas_call(
        functools.partial(_mix_kernel, alpha=alpha),
        grid=(n // tm,),
        in_specs=[
            pl.BlockSpec((tm, D_DELTA), row),
            pl.BlockSpec((tm, D_DELTA), row),
            pl.BlockSpec((tm, D_DELTA), lambda i: (i, 0)),
            pl.BlockSpec((tm, D_SGU), lambda i: (i, 1)),
            pl.BlockSpec((tm, D_SGU), lambda i: (i, 2)),
            pl.BlockSpec((tm, D_MODEL), row),
            pl.BlockSpec((1, HEAD_DIM), const2),
            pl.BlockSpec((1, D_SGU), const2),
            pl.BlockSpec((1, D_SGU), const2),
            pl.BlockSpec((N_GROUPS_SGU, SGU_CHUNK, SGU_CHUNK), const3),
            pl.BlockSpec((N_GROUPS_SGU, SGU_CHUNK, HEAD_DIM), const3),
            pl.BlockSpec((D_MODEL, D_MODEL), const2),
            pl.BlockSpec((1, D_MODEL), const2),
            pl.BlockSpec((1, D_MODEL), const2),
            pl.BlockSpec((N_EXPERTS, D_MODEL), const2),
            pl.BlockSpec((N_EXPERTS, 128), const2),
        ],
        out_specs=[
            pl.BlockSpec((tm * SLAB, LANES), row),
            pl.BlockSpec((8, tm), lambda i: (0, i)),
            pl.BlockSpec((N_EXPERTS, 128), const2),
        ],
        out_shape=[
            jax.ShapeDtypeStruct((n * SLAB, LANES), F32),
            jax.ShapeDtypeStruct((8, n), F32),
            jax.ShapeDtypeStruct((N_EXPERTS, 128), F32),
        ],
        compiler_params=pltpu.CompilerParams(
            dimension_semantics=("arbitrary",), vmem_limit_bytes=VMEM_LIMIT),
        name="mix",
    )(o_f, o_b, zuv, zuv, zuv, x2d, norm_w, sgu_g, sgu_b, w_sp, b_sp, w_out, ln_g, ln_b,
      w_router_t, router_bias)


def _dispatch_kernel(p1_ref, p2_ref, zt_ref, x_ref, xs_hbm, zero_buf, sem, zsem):
    i = pl.program_id(0)
    td = x_ref.shape[0] // SLAB
    zrows = zero_buf.shape[0]

    @pl.when(i == 0)
    def _():
        zero_buf[...] = jnp.zeros_like(zero_buf)
        copies = [
            pltpu.make_async_copy(
                zero_buf,
                xs_hbm.at[pl.ds(pl.multiple_of(jnp.maximum(zt_ref[j], 0) * zrows, zrows), zrows), :],
                zsem)
            for j in range(zt_ref.shape[0])]
        for j, c in enumerate(copies):
            pl.when(zt_ref[j] >= 0)(c.start)
        for j, c in enumerate(copies):
            pl.when(zt_ref[j] >= 0)(c.wait)

    base = i * td
    for r in range(td):
        src = x_ref.at[pl.ds(r * SLAB, SLAB), :]
        for k, p_ref in enumerate((p1_ref, p2_ref)):
            row = pl.multiple_of(p_ref[base + r] * SLAB, SLAB)
            pltpu.make_async_copy(src, xs_hbm.at[pl.ds(row, SLAB), :], sem).start(priority=k)
    for _ in range(2):
        pltpu.make_async_copy(x_ref, xs_hbm.at[pl.ds(0, td * SLAB), :], sem).wait()


def _dispatch(pos1, pos2, zero_tiles, x1s, total_rows, tm_e, td):
    n = x1s.shape[0] // SLAB
    grid_spec = pltpu.PrefetchScalarGridSpec(
        num_scalar_prefetch=3,
        grid=(n // td,),
        in_specs=[pl.BlockSpec((td * SLAB, LANES), lambda i, p1, p2, zt: (i, 0))],
        out_specs=pl.BlockSpec(memory_space=pl.ANY),
        scratch_shapes=[pltpu.VMEM((tm_e * SLAB, LANES), F32), pltpu.SemaphoreType.DMA(()),
                        pltpu.SemaphoreType.DMA(())],
    )
    return pl.pallas_call(
        _dispatch_kernel,
        grid_spec=grid_spec,
        out_shape=jax.ShapeDtypeStruct((total_rows * SLAB, LANES), F32),
        compiler_params=pltpu.CompilerParams(
            dimension_semantics=("arbitrary",), vmem_limit_bytes=VMEM_LIMIT),
        name="dispatch",
    )(pos1, pos2, zero_tiles, x1s)


def _expert_kernel(te_ref, nt_ref, xs_ref, wg_ref, wu_ref, wd_ref, y_ref, wgb, wub, wdb):
    i = pl.program_id(0)
    tm = xs_ref.shape[0] // SLAB
    ntiles = nt_ref[0]

    @pl.when(i < ntiles)
    def _():
        @pl.when((i == 0) | (te_ref[i] != te_ref[jnp.maximum(i - 1, 0)]))
        def _():
            wgb[...] = wg_ref[...].astype(BF16)
            wub[...] = wu_ref[...].astype(BF16)
            wdb[...] = wd_ref[...].astype(BF16)

        xs = jnp.concatenate(_load_slabs(xs_ref, tm), -1).astype(BF16)
        hg = jnp.dot(xs, wgb[...], preferred_element_type=F32)
        hu = jnp.dot(xs, wub[...], preferred_element_type=F32)
        hid = (hg * jax.nn.sigmoid(hg) * hu).astype(BF16)
        _store_slabs(y_ref, jnp.dot(hid, wdb[...], preferred_element_type=F32))

    @pl.when(i >= ntiles)
    def _():
        y_ref[...] = jnp.zeros_like(y_ref)


def _experts(xs, tile_expert, ntiles, w_gate, w_up, w_down, layer, tm):
    max_tiles = tile_expert.shape[0]

    def wmap(i, te, nt):
        return (layer, te[jnp.minimum(i, nt[0] - 1)], 0, 0)

    wspec = pl.BlockSpec((None, None, D_MODEL, D_MODEL), wmap)
    grid_spec = pltpu.PrefetchScalarGridSpec(
        num_scalar_prefetch=2,
        grid=(max_tiles,),
        in_specs=[pl.BlockSpec((tm * SLAB, LANES), lambda i, te, nt: (jnp.minimum(i, nt[0] - 1), 0)),
                  wspec, wspec, wspec],
        out_specs=pl.BlockSpec((tm * SLAB, LANES), lambda i, te, nt: (i, 0)),
        scratch_shapes=[pltpu.VMEM((D_MODEL, D_MODEL), BF16)] * 3,
    )
    return pl.pallas_call(
        _expert_kernel,
        grid_spec=grid_spec,
        out_shape=jax.ShapeDtypeStruct(xs.shape, F32),
        compiler_params=pltpu.CompilerParams(
            dimension_semantics=("arbitrary",), vmem_limit_bytes=VMEM_LIMIT),
        name="experts",
    )(tile_expert, ntiles, xs, w_gate, w_up, w_down)


def _combine_kernel(p1_ref, p2_ref, y_hbm, x1_ref, gate_ref, lg_ref, lb_ref, o_ref, buf, sem,
                    *, alpha):
    i = pl.program_id(0)
    n_steps = pl.num_programs(0)
    tm = o_ref.shape[0]

    def gather(tile, slot):
        base = tile * tm
        for r in range(tm):
            for k, p_ref in enumerate((p1_ref, p2_ref)):
                row = pl.multiple_of(p_ref[base + r] * SLAB, SLAB)
                pltpu.make_async_copy(y_hbm.at[pl.ds(row, SLAB), :],
                                      buf.at[slot, k, pl.ds(r * SLAB, SLAB), :],
                                      sem.at[slot]).start(priority=k)

    def wait(slot):
        for k in range(2):
            pltpu.make_async_copy(y_hbm.at[pl.ds(0, tm * SLAB), :], buf.at[slot, k],
                                  sem.at[slot]).wait()

    @pl.when(i == 0)
    def _():
        gather(0, 0)

    def step(slot):
        wait(slot)
        gather(jnp.minimum(i + 1, n_steps - 1), 1 - slot)
        gates = gate_ref[...]
        g1, g2 = gates[:, 4:5], gates[:, 5:6]
        x1 = _load_slabs(x1_ref, tm)
        y1 = _load_slabs(buf.at[slot, 0], tm)
        y2 = _load_slabs(buf.at[slot, 1], tm)
        y = jnp.concatenate([alpha * x1[s] + g1 * y1[s] + g2 * y2[s] for s in range(SLAB)], -1)
        o_ref[...] = _layer_norm(y, lg_ref[...], lb_ref[...])

        @pl.when(i == n_steps - 1)
        def _():
            wait(1 - slot)

    for parity in range(2):
        pl.when(i % 2 == parity)(functools.partial(step, parity))


def _combine(pos1, pos2, ys, x1, gates, ln_g, ln_b, alpha, tm):
    n = x1.shape[0] // SLAB
    grid_spec = pltpu.PrefetchScalarGridSpec(
        num_scalar_prefetch=2,
        grid=(n // tm,),
        in_specs=[
            pl.BlockSpec(memory_space=pl.ANY),
            pl.BlockSpec((tm * SLAB, LANES), lambda i, p1, p2: (i, 0)),
            pl.BlockSpec((tm, 8), lambda i, p1, p2: (i, 0)),
            pl.BlockSpec((1, D_MODEL), lambda i, p1, p2: (0, 0)),
            pl.BlockSpec((1, D_MODEL), lambda i, p1, p2: (0, 0)),
        ],
        out_specs=pl.BlockSpec((tm, D_MODEL), lambda i, p1, p2: (i, 0)),
        scratch_shapes=[pltpu.VMEM((2, 2, tm * SLAB, LANES), F32), pltpu.SemaphoreType.DMA((2,))],
    )
    return pl.pallas_call(
        functools.partial(_combine_kernel, alpha=alpha),
        grid_spec=grid_spec,
        out_shape=jax.ShapeDtypeStruct((n, D_MODEL), F32),
        compiler_params=pltpu.CompilerParams(
            dimension_semantics=("arbitrary",), vmem_limit_bytes=VMEM_LIMIT),
        name="combine",
    )(pos1, pos2, ys, x1, gates, ln_g, ln_b)


def _combine_proj_kernel(p1_ref, p2_ref, y_hbm, x1c_ref, x1p_ref, x1n_ref, gc_ref, gp_ref, gn_ref,
                         lg_ref, lb_ref, wqkv_ref, wzuv_ref, wgl_ref, cw_ref, gpar_ref,
                         x2_ref, qkv_ref, zuv_ref, gcol_ref, buf, sem, xe_ref,
                         *, alpha, tiles_per_seq, col_chunk, n_tok):
    i = pl.program_id(0)
    n_steps = pl.num_programs(0)
    h = HALO_ROWS
    tm = x2_ref.shape[0]
    ext = tm + 2 * h

    def gather(tile, slot):
        first = tile * tm - h
        for r in range(ext):
            tok = first + r
            if r < h:
                tok = jnp.maximum(tok, 0)
            elif r >= h + tm:
                tok = jnp.minimum(tok, n_tok - 1)
            for k, p_ref in enumerate((p1_ref, p2_ref)):
                row = pl.multiple_of(p_ref[tok] * SLAB, SLAB)
                pltpu.make_async_copy(y_hbm.at[pl.ds(row, SLAB), :],
                                      buf.at[slot, k, pl.ds(r * SLAB, SLAB), :],
                                      sem.at[slot]).start(priority=k)

    def wait(slot):
        for k in range(2):
            pltpu.make_async_copy(y_hbm.at[pl.ds(0, ext * SLAB), :], buf.at[slot, k],
                                  sem.at[slot]).wait()

    @pl.when(i == 0)
    def _():
        gather(0, 0)

    slot = i % 2
    wait(slot)
    nxt = jnp.minimum(i + 1, n_steps - 1)
    for parity in range(2):
        pl.when(slot == parity)(functools.partial(gather, nxt, 1 - parity))

    gates = jnp.concatenate([gp_ref[...], gc_ref[...], gn_ref[...]], 0)
    g1, g2 = gates[:, 4:5], gates[:, 5:6]
    x1p, x1c, x1n = _load_slabs(x1p_ref, h), _load_slabs(x1c_ref, tm), _load_slabs(x1n_ref, h)
    y1 = _load_slabs(buf.at[slot, 0], ext)
    y2 = _load_slabs(buf.at[slot, 1], ext)
    y = jnp.concatenate(
        [alpha * jnp.concatenate([x1p[s], x1c[s], x1n[s]], 0) + g1 * y1[s] + g2 * y2[s]
         for s in range(SLAB)], -1)
    x2 = _layer_norm(y, lg_ref[...], lb_ref[...])
    x2_ref[...] = x2[h:h + tm]
    pos = i % tiles_per_seq
    xe_ref[0:h, :] = jnp.where(pos == 0, 0.0, x2[:h]).astype(BF16)
    xe_ref[h:h + tm, :] = x2[h:h + tm].astype(BF16)
    xe_ref[h + tm:, :] = jnp.where(pos == tiles_per_seq - 1, 0.0, x2[h + tm:]).astype(BF16)
    _proj_body(xe_ref, wqkv_ref, wzuv_ref, wgl_ref, cw_ref, gpar_ref, qkv_ref, zuv_ref, gcol_ref,
               col_chunk)

    @pl.when(i == n_steps - 1)
    def _():
        wait(1 - slot)


def _combine_proj(pos1, pos2, ys, x1, gates, ln_g, ln_b, w_qkv, w_zuv, w_gl, conv_w8,
                  gate_params, alpha, seq, tm):
    n = x1.shape[0] // SLAB
    hb = tm // HALO_ROWS
    last_hblk = n // HALO_ROWS - 1
    cur = lambda i, p1, p2: (i, 0)
    prev = lambda i, p1, p2: (jnp.maximum(i * hb - 1, 0), 0)
    nxt = lambda i, p1, p2: (jnp.minimum((i + 1) * hb, last_hblk), 0)
    const = lambda i, p1, p2: (0, 0)
    ext = tm + 2 * HALO_ROWS
    grid_spec = pltpu.PrefetchScalarGridSpec(
        num_scalar_prefetch=2,
        grid=(n // tm,),
        in_specs=[
            pl.BlockSpec(memory_space=pl.ANY),
            pl.BlockSpec((tm * SLAB, LANES), cur),
            pl.BlockSpec((HALO_ROWS * SLAB, LANES), prev),
            pl.BlockSpec((HALO_ROWS * SLAB, LANES), nxt),
            pl.BlockSpec((tm, 8), cur),
            pl.BlockSpec((HALO_ROWS, 8), prev),
            pl.BlockSpec((HALO_ROWS, 8), nxt),
            pl.BlockSpec((1, D_MODEL), const),
            pl.BlockSpec((1, D_MODEL), const),
            pl.BlockSpec((D_MODEL, 3 * D_DELTA), const),
            pl.BlockSpec((D_MODEL, 3 * D_DELTA), const),
            pl.BlockSpec((D_MODEL, GATE_COLS), const),
            pl.BlockSpec((8, 3 * D_DELTA), const),
            pl.BlockSpec((8, GATE_COLS), const),
        ],
        out_specs=[
            pl.BlockSpec((tm, D_MODEL), cur),
            pl.BlockSpec((tm, 3 * D_DELTA), cur),
            pl.BlockSpec((tm, 3 * D_DELTA), cur),
            pl.BlockSpec((tm, 4 * N_HEADS), cur),
        ],
        scratch_shapes=[pltpu.VMEM((2, 2, ext * SLAB, LANES), F32), pltpu.SemaphoreType.DMA((2,)),
                        pltpu.VMEM((ext, D_MODEL), BF16)],
    )
    return pl.pallas_call(
        functools.partial(_combine_proj_kernel, alpha=alpha, tiles_per_seq=seq // tm,
                          col_chunk=4 * HEAD_DIM, n_tok=n),
        grid_spec=grid_spec,
        out_shape=[
            jax.ShapeDtypeStruct((n, D_MODEL), F32),
            jax.ShapeDtypeStruct((n, 3 * D_DELTA), BF16),
            jax.ShapeDtypeStruct((n, 3 * D_DELTA), BF16),
            jax.ShapeDtypeStruct((n, 4 * N_HEADS), F32),
        ],
        compiler_params=pltpu.CompilerParams(
            dimension_semantics=("arbitrary",), vmem_limit_bytes=VMEM_LIMIT),
        name="combine_proj",
    )(pos1, pos2, ys, x1, x1, x1, gates, gates, gates, ln_g, ln_b, w_qkv, w_zuv, w_gl, conv_w8,
      gate_params)


def _routing_tables(meta, counts, tm_e):
    n = meta.shape[1]
    max_tiles = (2 * n) // tm_e + N_EXPERTS
    cnt = counts[:, 0].astype(jnp.int32)
    tiles = (cnt + tm_e - 1) // tm_e
    tile_end = jnp.cumsum(tiles)
    row_start = (tile_end - tiles) * tm_e
    e1 = meta[0].astype(jnp.int32)
    e2 = meta[1].astype(jnp.int32)
    pos1 = row_start[e1] + meta[2].astype(jnp.int32)
    pos2 = row_start[e2] + meta[3].astype(jnp.int32)
    ntiles = tile_end[-1:]
    tile_ids = jnp.arange(max_tiles, dtype=jnp.int32)
    tile_expert = jnp.minimum(
        jnp.sum((tile_ids[:, None] >= tile_end[None, :]).astype(jnp.int32), 1), N_EXPERTS - 1)
    tail = ntiles[0] + jnp.arange(N_EXPERTS, dtype=jnp.int32)
    zero_tiles = jnp.concatenate([
        jnp.where(tiles > 0, tile_end - 1, -1), jnp.where(tail < max_tiles, tail, -1)])
    return pos1, pos2, tile_expert, ntiles.astype(jnp.int32), zero_tiles.astype(jnp.int32)


def _forward(x, w_in, conv_w, a_log_f, dt_bias_f, a_log_b, dt_bias_b, delta_norm_w,
             sgu_ln_g, sgu_ln_b, w_spatial, b_spatial, w_out, ln1_g, ln1_b,
             w_router, router_bias, w_gate, w_up, w_down, ln2_g, ln2_b,
             *, tm_proj, bb_delta, tm_mix, td_disp, tm_exp, tm_comb):
    batch, seq, d = x.shape
    depth = w_in.shape[0]
    n = batch * seq
    alpha = (2.0 * depth) ** 0.25
    x2d = x.reshape(n, d)

    g0 = 4 * D_DELTA
    w_qkv = w_in[:, :, :3 * D_DELTA].astype(BF16)
    w_zuv = jnp.concatenate([w_in[:, :, 3 * D_DELTA:g0], w_in[:, :, g0 + 16:]], -1).astype(BF16)
    w_gl = jnp.pad(w_in[:, :, g0:g0 + 16], ((0, 0), (0, 0), (0, GATE_COLS - 16))).astype(BF16)
    conv_w8 = jnp.pad(conv_w, ((0, 0), (0, 8 - CONV_WIDTH), (0, 0)))
    pad_lanes = lambda a, b: jnp.pad(jnp.concatenate([a, b], -1), ((0, 0), (0, GATE_COLS - 8)))
    gate_params = jnp.pad(
        jnp.stack([pad_lanes(a_log_f, a_log_b), pad_lanes(dt_bias_f, dt_bias_b)], 1),
        ((0, 0), (0, 6), (0, 0)))
    w_sp = w_spatial.astype(BF16)
    b_sp = jnp.broadcast_to(b_spatial[..., None], b_spatial.shape + (HEAD_DIM,))
    w_out_b = w_out.astype(BF16)
    w_router_t = w_router.T
    rb = jnp.broadcast_to(router_bias[:, None], (N_EXPERTS, 128))
    assert (2 * n) % tm_exp == 0
    max_tiles = (2 * n) // tm_exp + N_EXPERTS

    qkv, zuv, gcol = _proj(x2d, w_qkv[0], w_zuv[0], w_gl[0], conv_w8[0], gate_params[0], seq,
                           tm_proj)
    for l in range(depth):
        gcol3 = gcol.reshape(batch, seq, 4 * N_HEADS)
        grow = gcol.reshape(batch, seq // CHUNK, CHUNK, 4 * N_HEADS).transpose(0, 1, 3, 2)
        o_f, o_b = _delta(qkv.reshape(batch, seq, 3 * D_DELTA), gcol3, grow, batch, seq, bb_delta)
        o_f, o_b = o_f.reshape(n, D_DELTA), o_b.reshape(n, D_DELTA)
        x1, meta, counts = _mix(
            o_f, o_b, zuv, x2d, delta_norm_w[l][None], sgu_ln_g[l][None], sgu_ln_b[l][None],
            w_sp[l], b_sp[l], w_out_b[l], ln1_g[l][None], ln1_b[l][None], w_router_t, rb,
            alpha, tm_mix)
        pos1, pos2, tile_expert, ntiles, zero_tiles = _routing_tables(meta, counts, tm_exp)
        xs = _dispatch(pos1, pos2, zero_tiles, x1, max_tiles * tm_exp, tm_exp, td_disp)
        ys = _experts(xs, tile_expert, ntiles, w_gate, w_up, w_down, l, tm_exp)
        if l + 1 < depth:
            x2d, qkv, zuv, gcol = _combine_proj(
                pos1, pos2, ys, x1, meta.T, ln2_g[l][None], ln2_b[l][None], w_qkv[l + 1],
                w_zuv[l + 1], w_gl[l + 1], conv_w8[l + 1], gate_params[l + 1], alpha, seq, tm_proj)
        else:
            x2d = _combine(pos1, pos2, ys, x1, meta.T, ln2_g[l][None], ln2_b[l][None], alpha,
                           tm_comb)
    return x2d.reshape(batch, seq, d)


def kernel(x, w_in, conv_w, a_log_f, dt_bias_f, a_log_b, dt_bias_b, delta_norm_w, sgu_ln_g,
           sgu_ln_b, w_spatial, b_spatial, w_out, ln1_g, ln1_b, w_router, router_bias, w_gate,
           w_up, w_down, ln2_g, ln2_b):
    return _forward(x, w_in, conv_w, a_log_f, dt_bias_f, a_log_b, dt_bias_b, delta_norm_w,
                    sgu_ln_g, sgu_ln_b, w_spatial, b_spatial, w_out, ln1_g, ln1_b, w_router,
                    router_bias, w_gate, w_up, w_down, ln2_g, ln2_b,
                    tm_proj=512, bb_delta=4, tm_mix=512, td_disp=1024, tm_exp=512,
                    tm_comb=256)
```

```python
import functools

import jax
import jax.numpy as jnp
from jax import lax
from jax.experimental import pallas as pl
from jax.experimental.pallas import tpu as pltpu

F32 = jnp.float32
BF16 = jnp.bfloat16

D_MODEL = 1024
D_DELTA = 512
D_SGU = 512
HEAD_DIM = 128
N_HEADS = 4
CONV_WIDTH = 5
CHUNK = 64
SGU_CHUNK = 128
N_GROUPS_SGU = 4
N_EXPERTS = 16
N_EXPERT_GROUPS = 4
EXPERTS_PER_GROUP = 4
LN_EPS = 1e-5
RMS_EPS = 1e-6
L2_EPS = 1e-6

GATE_COLS = 128
HALO_ROWS = 16
LANES = 128
SLAB = D_MODEL // LANES
VMEM_LIMIT = 56 * 1024 * 1024


def _bdot(a, b):
    return jnp.dot(a.astype(BF16), b.astype(BF16), preferred_element_type=F32)


def _bdot_nt(a, b):
    return lax.dot_general(a.astype(BF16), b.astype(BF16), (((1,), (1,)), ((), ())),
                           preferred_element_type=F32)


def _bdot_tn(a, b):
    return lax.dot_general(a.astype(BF16), b.astype(BF16), (((0,), (0,)), ((), ())),
                           preferred_element_type=F32)


def _layer_norm(y, g, b):
    mu = jnp.mean(y, -1, keepdims=True)
    d = y - mu
    var = jnp.mean(d * d, -1, keepdims=True)
    return d * lax.rsqrt(var + LN_EPS) * g + b


def _load_slabs(ref, rows):
    return [ref[pl.ds(s, rows, stride=SLAB), :] for s in range(SLAB)]


def _store_slabs(ref, mat):
    rows = mat.shape[0]
    for s in range(SLAB):
        ref[pl.ds(s, rows, stride=SLAB), :] = mat[:, s * LANES:(s + 1) * LANES]


def _conv_taps(ext, w, lo, n):
    rows = ext.shape[0]
    acc = None
    for j in range(CONV_WIDTH):
        shift = (-(j - CONV_WIDTH // 2)) % rows
        r = ext if shift == 0 else pltpu.roll(ext, shift, 0)
        term = r[lo:lo + n] * w[j:j + 1]
        acc = term if acc is None else acc + term
    return acc


def _conv_post(y, first_head):
    s = y * jax.nn.sigmoid(y)
    outs = []
    for j in range(y.shape[1] // HEAD_DIM):
        hh = first_head + j
        seg = s[:, j * HEAD_DIM:(j + 1) * HEAD_DIM]
        if hh < 2 * N_HEADS:
            ss = jnp.sum(seg * seg, -1, keepdims=True)
            seg = seg * lax.rsqrt(ss + L2_EPS)
            if hh < N_HEADS:
                seg = seg * (HEAD_DIM ** -0.5)
        outs.append(seg)
    return jnp.concatenate(outs, -1).astype(BF16)


def _proj_kernel(cur_ref, prev_ref, next_ref, wqkv_ref, wzuv_ref, wgl_ref, cw_ref, gp_ref,
                 qkv_ref, zuv_ref, gcol_ref, xe_ref, *, tiles_per_seq, col_chunk):
    i = pl.program_id(0)
    tb = cur_ref.shape[0]
    h = HALO_ROWS
    pos = i % tiles_per_seq
    xc = cur_ref[...].astype(BF16)
    xp = jnp.where(pos == 0, 0.0, prev_ref[...]).astype(BF16)
    xn = jnp.where(pos == tiles_per_seq - 1, 0.0, next_ref[...]).astype(BF16)
    xe_ref[0:h, :] = xp
    xe_ref[h:h + tb, :] = xc
    xe_ref[h + tb:, :] = xn
    n_chunks = 3 * D_DELTA // col_chunk
    cols = [slice(c * col_chunk, (c + 1) * col_chunk) for c in range(n_chunks)]

    def conv(c, h_ext):
        y = _conv_taps(h_ext, cw_ref[:, cols[c]], h, tb)
        qkv_ref[:, cols[c]] = _conv_post(y, c * col_chunk // HEAD_DIM)

    h_prev = jnp.dot(xe_ref[...], wqkv_ref[:, cols[0]], preferred_element_type=F32)
    for c in range(1, n_chunks):
        h_cur = jnp.dot(xe_ref[...], wqkv_ref[:, cols[c]], preferred_element_type=F32)
        conv(c - 1, h_prev)
        h_prev = h_cur
    for c in range(n_chunks):
        zuv_ref[:, cols[c]] = jnp.dot(xe_ref[h:h + tb, :], wzuv_ref[:, cols[c]],
                                      preferred_element_type=F32).astype(BF16)
        if c == 0:
            conv(n_chunks - 1, h_prev)

    gl = jnp.dot(xe_ref[h:h + tb, :], wgl_ref[...], preferred_element_type=F32)
    neg_a = -jnp.exp(gp_ref[0:1, :])
    xg = gl + gp_ref[1:2, :]
    softplus = jnp.maximum(xg, 0.0) + jnp.log1p(jnp.exp(-jnp.abs(xg)))
    g = neg_a * softplus
    beta = jax.nn.sigmoid(gl)
    lane = lax.broadcasted_iota(jnp.int32, (CHUNK, GATE_COLS), 1)
    r = lax.broadcasted_iota(jnp.int32, (2 * CHUNK, CHUNK), 0)
    c = lax.broadcasted_iota(jnp.int32, (2 * CHUNK, CHUNK), 1)
    tri = jnp.where(r < CHUNK, jnp.where(r >= c, 1.0, 0.0), jnp.where(r - CHUNK <= c, 1.0, 0.0))
    for ch in range(tb // CHUNK):
        sl = slice(ch * CHUNK, (ch + 1) * CHUNK)
        cs = jnp.dot(tri, g[sl], precision=lax.Precision.HIGHEST, preferred_element_type=F32)
        out = jnp.where(lane < N_HEADS, cs[:CHUNK],
                        jnp.where(lane < 2 * N_HEADS, cs[CHUNK:], beta[sl]))
        gcol_ref[sl, :] = out[:, :4 * N_HEADS]


def _proj(x2d, w_qkv, w_zuv, w_gl, conv_w8, gate_params, seq, tb):
    n = x2d.shape[0]
    hb = tb // HALO_ROWS
    last_hblk = n // HALO_ROWS - 1
    const = lambda i: (0, 0)
    return pl.pallas_call(
        functools.partial(_proj_kernel, tiles_per_seq=seq // tb, col_chunk=4 * HEAD_DIM),
        grid=(n // tb,),
        in_specs=[
            pl.BlockSpec((tb, D_MODEL), lambda i: (i, 0)),
            pl.BlockSpec((HALO_ROWS, D_MODEL), lambda i: (jnp.maximum(i * hb - 1, 0), 0)),
            pl.BlockSpec((HALO_ROWS, D_MODEL), lambda i: (jnp.minimum((i + 1) * hb, last_hblk), 0)),
            pl.BlockSpec((D_MODEL, 3 * D_DELTA), const),
            pl.BlockSpec((D_MODEL, 3 * D_DELTA), const),
            pl.BlockSpec((D_MODEL, GATE_COLS), const),
            pl.BlockSpec((8, 3 * D_DELTA), const),
            pl.BlockSpec((8, GATE_COLS), const),
        ],
        out_specs=[
            pl.BlockSpec((tb, 3 * D_DELTA), lambda i: (i, 0)),
            pl.BlockSpec((tb, 3 * D_DELTA), lambda i: (i, 0)),
            pl.BlockSpec((tb, 4 * N_HEADS), lambda i: (i, 0)),
        ],
        out_shape=[
            jax.ShapeDtypeStruct((n, 3 * D_DELTA), BF16),
            jax.ShapeDtypeStruct((n, 3 * D_DELTA), BF16),
            jax.ShapeDtypeStruct((n, 4 * N_HEADS), F32),
        ],
        scratch_shapes=[pltpu.VMEM((tb + 2 * HALO_ROWS, D_MODEL), BF16)],
        compiler_params=pltpu.CompilerParams(
            dimension_semantics=("parallel",), vmem_limit_bytes=VMEM_LIMIT),
        name="proj",
    )(x2d, x2d, x2d, w_qkv, w_zuv, w_gl, conv_w8, gate_params)


def _delta_chains(chains):
    c, nh = CHUNK, N_HEADS
    hc = nh * c
    n_ch = len(chains)
    every = range(n_ch)

    ri = lax.broadcasted_iota(jnp.int32, (hc, hc), 0)
    ci = lax.broadcasted_iota(jnp.int32, (hc, hc), 1)
    same64 = (ri >> 6) == (ci >> 6)
    same32 = (ri >> 5) == (ci >> 5)
    same16 = (ri >> 4) == (ci >> 4)
    eye = jnp.where(ri == ci, 1.0, 0.0)
    incl_of = {True: same64 & (ri >= ci), False: same64 & (ri <= ci)}
    strict_of = {d: m & (ri != ci) for d, m in incl_of.items()}

    q, k, v, gc_col, beta, g_last, k_dec, qg, decay, incl = ([] for _ in range(10))
    for qkv, gcol, grow, _, _, fwd in chains:
        off = 0 if fwd else nh

        def stack(base):
            return jnp.concatenate(
                [qkv[:, (base + h) * HEAD_DIM:(base + h + 1) * HEAD_DIM] for h in range(nh)], 0
            ).astype(F32)

        q.append(stack(0))
        k.append(stack(nh))
        v.append(stack(2 * nh))
        col = jnp.concatenate([gcol[:, off + h:off + h + 1] for h in range(nh)], 0)
        row = jnp.concatenate([grow[off + h:off + h + 1, :] for h in range(nh)], 1)
        gc_col.append(col)
        beta.append(jnp.concatenate(
            [gcol[:, 2 * nh + off + h:2 * nh + off + h + 1] for h in range(nh)], 0))
        edge = c - 1 if fwd else 0
        gl = [gcol[edge:edge + 1, off + h:off + h + 1] for h in range(nh)]
        g_last.append(gl)
        gl_col = jnp.concatenate([jnp.broadcast_to(g, (c, 1)) for g in gl], 0)
        k_dec.append(k[-1] * jnp.exp(gl_col - col))
        qg.append(q[-1] * jnp.exp(col))
        m = incl_of[fwd]
        incl.append(m)
        decay.append(jnp.where(m, jnp.exp(jnp.where(m, col - row, 0.0)), 0.0))

    kb = [k[i] * beta[i] for i in every]
    a = [jnp.where(strict_of[chains[i][5]], _bdot_nt(kb[i], k[i]) * decay[i], 0.0) for i in every]
    qk = [jnp.where(incl[i], _bdot_nt(q[i], k[i]) * decay[i], 0.0).astype(BF16) for i in every]

    p = [jnp.where(same16, a[i], 0.0) for i in every]
    t = [eye - p[i] for i in every]
    p = [x.astype(BF16) for x in p]
    tb = [x.astype(BF16) for x in t]
    for _ in range(3):
        p = [_bdot(p[i], p[i]).astype(BF16) for i in every]
        t = [t[i] + _bdot(tb[i], p[i]) for i in every]
        tb = [x.astype(BF16) for x in t]
    a32 = [jnp.where(same32, jnp.where(same16, 0.0, a[i]), 0.0).astype(BF16) for i in every]
    a64 = [jnp.where(same32, 0.0, a[i]).astype(BF16) for i in every]
    for a_off in (a32, a64):
        m = [_bdot(a_off[i], tb[i]).astype(BF16) for i in every]
        t = [t[i] - _bdot(tb[i], m[i]) for i in every]
        tb = [x.astype(BF16) for x in t]

    uw = [_bdot(tb[i], jnp.concatenate([v[i] * beta[i], kb[i] * jnp.exp(gc_col[i])], 1))
          for i in every]

    ws = [[_bdot(jnp.concatenate([uw[i][h * c:(h + 1) * c, HEAD_DIM:],
                                  qg[i][h * c:(h + 1) * c]], 0), chains[i][3][h])
           for h in range(nh)] for i in every]
    v_new = [[uw[i][h * c:(h + 1) * c, :HEAD_DIM] - ws[i][h][:c] for h in range(nh)]
             for i in every]
    o = [jnp.concatenate([ws[i][h][c:] for h in range(nh)], 0)
         + _bdot(qk[i], jnp.concatenate(v_new[i], 0)) for i in every]
    for i in every:
        s_ref, o_ref = chains[i][3], chains[i][4]
        for h in range(nh):
            rows = slice(h * c, (h + 1) * c)
            s_ref[h] = (s_ref[h] * jnp.exp(g_last[i][h])
                        + _bdot_tn(k_dec[i][rows], v_new[i][h]))
            o_ref[:, h * HEAD_DIM:(h + 1) * HEAD_DIM] = o[i][rows]


def _delta_kernel(qkv_f_ref, gcol_f_ref, grow_f_ref, qkv_b_ref, gcol_b_ref, grow_b_ref,
                  of_ref, ob_ref, sf_ref, sb_ref):
    @pl.when(pl.program_id(1) == 0)
    def _():
        sf_ref[...] = jnp.zeros_like(sf_ref)
        sb_ref[...] = jnp.zeros_like(sb_ref)

    n_sub = grow_f_ref.shape[1]
    for step in range(n_sub):
        cf, cb = step, n_sub - 1 - step
        rows_f, rows_b = pl.ds(cf * CHUNK, CHUNK), pl.ds(cb * CHUNK, CHUNK)
        chains = []
        for bb in range(qkv_f_ref.shape[0]):
            chains.append((qkv_f_ref[bb, rows_f, :], gcol_f_ref[bb, rows_f, :], grow_f_ref[bb, cf],
                           sf_ref.at[bb], of_ref.at[bb, rows_f, :], True))
            chains.append((qkv_b_ref[bb, rows_b, :], gcol_b_ref[bb, rows_b, :], grow_b_ref[bb, cb],
                           sb_ref.at[bb], ob_ref.at[bb, rows_b, :], False))
        _delta_chains(chains)


def _delta(qkv, gcol, grow, batch, seq, bb, n_sub):
    nc = seq // (CHUNK * n_sub)
    rows = CHUNK * n_sub
    fwd = lambda b, j: (b, j, 0)
    bwd = lambda b, j: (b, nc - 1 - j, 0)
    fwd4 = lambda b, j: (b, j, 0, 0)
    bwd4 = lambda b, j: (b, nc - 1 - j, 0, 0)
    return pl.pallas_call(
        _delta_kernel,
        grid=(batch // bb, nc),
        in_specs=[
            pl.BlockSpec((bb, rows, 3 * D_DELTA), fwd),
            pl.BlockSpec((bb, rows, 4 * N_HEADS), fwd),
            pl.BlockSpec((bb, n_sub, 4 * N_HEADS, CHUNK), fwd4),
            pl.BlockSpec((bb, rows, 3 * D_DELTA), bwd),
            pl.BlockSpec((bb, rows, 4 * N_HEADS), bwd),
            pl.BlockSpec((bb, n_sub, 4 * N_HEADS, CHUNK), bwd4),
        ],
        out_specs=[
            pl.BlockSpec((bb, rows, D_DELTA), fwd),
            pl.BlockSpec((bb, rows, D_DELTA), bwd),
        ],
        out_shape=[
            jax.ShapeDtypeStruct((batch, seq, D_DELTA), F32),
            jax.ShapeDtypeStruct((batch, seq, D_DELTA), F32),
        ],
        scratch_shapes=[
            pltpu.VMEM((bb, N_HEADS, HEAD_DIM, HEAD_DIM), F32),
            pltpu.VMEM((bb, N_HEADS, HEAD_DIM, HEAD_DIM), F32),
        ],
        compiler_params=pltpu.CompilerParams(
            dimension_semantics=("arbitrary", "arbitrary"), vmem_limit_bytes=VMEM_LIMIT),
        name="delta",
    )(qkv, gcol, grow, qkv, gcol, grow)


def _gelu(x):
    return 0.5 * x * (1.0 + lax.erf(x * (2.0 ** -0.5)))


def _split_bf16(x):
    hi = x.astype(BF16)
    lo = (x - hi.astype(F32)).astype(BF16)
    return hi, lo


def _mix_kernel(of_ref, ob_ref, z_ref, u_ref, v_ref, x_ref, nw_ref, sg_ref, sb_ref, wsp_ref,
                bsp_ref, wout_ref, lg_ref, lb_ref, wr_ref, rb_ref,
                x1_ref, meta_ref, cnt_ref, *, alpha):
    tm = x_ref.shape[0]

    @pl.when(pl.program_id(0) == 0)
    def _():
        cnt_ref[...] = jnp.zeros_like(cnt_ref)

    o = of_ref[...] + ob_ref[...]
    z = z_ref[...].astype(F32)
    nw = nw_ref[...]
    parts = []
    for h in range(N_HEADS):
        sl = slice(h * HEAD_DIM, (h + 1) * HEAD_DIM)
        seg = o[:, sl]
        seg = seg * lax.rsqrt(jnp.mean(seg * seg, -1, keepdims=True) + RMS_EPS) * nw
        zs = z[:, sl]
        parts.append((seg * (zs * jax.nn.sigmoid(zs))).astype(BF16))

    u = _gelu(u_ref[...].astype(F32))
    v = _layer_norm(_gelu(v_ref[...].astype(F32)), sg_ref[...], sb_ref[...]).astype(BF16)
    for g in range(N_GROUPS_SGU):
        sl = slice(g * HEAD_DIM, (g + 1) * HEAD_DIM)
        w_g = wsp_ref[g]
        blocks = []
        for ch in range(tm // SGU_CHUNK):
            rows = slice(ch * SGU_CHUNK, (ch + 1) * SGU_CHUNK)
            mixed = jnp.dot(w_g, v[rows, sl], preferred_element_type=F32) + bsp_ref[g]
            blocks.append(u[rows, sl] * mixed)
        parts.append(jnp.concatenate(blocks, 0).astype(BF16))

    mix = jnp.dot(jnp.concatenate(parts, -1), wout_ref[...], preferred_element_type=F32)
    x1 = _layer_norm(alpha * x_ref[...] + mix, lg_ref[...], lb_ref[...])
    _store_slabs(x1_ref, x1)

    x_hi, x_lo = _split_bf16(x1)
    w_hi, w_lo = _split_bf16(wr_ref[...])
    nt = (((1,), (1,)), ((), ()))
    logits = (lax.dot_general(w_hi, x_hi, nt, preferred_element_type=F32)
              + lax.dot_general(w_lo, x_hi, nt, preferred_element_type=F32)
              + lax.dot_general(w_hi, x_lo, nt, preferred_element_type=F32))
    scores = jax.nn.sigmoid(logits)
    sel = scores + rb_ref[:, 0:1]

    def row(m, e):
        return m[e:e + 1, :]

    best = jnp.zeros((1, tm), jnp.int32)
    best_score = None
    for gi in range(N_EXPERT_GROUPS):
        vals = [row(sel, gi * EXPERTS_PER_GROUP + j) for j in range(EXPERTS_PER_GROUP)]
        top2 = None
        for i in range(EXPERTS_PER_GROUP):
            for j in range(i + 1, EXPERTS_PER_GROUP):
                pair = vals[i] + vals[j]
                top2 = pair if top2 is None else jnp.maximum(top2, pair)
        if gi == 0:
            best_score = top2
        else:
            better = top2 > best_score
            best = jnp.where(better, gi, best)
            best_score = jnp.where(better, top2, best_score)

    def pick(m, j):
        out = row(m, j)
        for gi in range(1, N_EXPERT_GROUPS):
            out = jnp.where(best == gi, row(m, gi * EXPERTS_PER_GROUP + j), out)
        return out

    cand = [pick(sel, j) for j in range(EXPERTS_PER_GROUP)]
    raw = [pick(scores, j) for j in range(EXPERTS_PER_GROUP)]
    i1, m1, s1 = jnp.zeros((1, tm), jnp.int32), cand[0], raw[0]
    for j in range(1, EXPERTS_PER_GROUP):
        gt = cand[j] > m1
        i1 = jnp.where(gt, j, i1)
        m1 = jnp.where(gt, cand[j], m1)
        s1 = jnp.where(gt, raw[j], s1)
    i2 = jnp.zeros((1, tm), jnp.int32)
    m2 = jnp.full((1, tm), -jnp.inf, F32)
    s2 = jnp.zeros((1, tm), F32)
    for j in range(EXPERTS_PER_GROUP):
        cj = jnp.where(i1 == j, -jnp.inf, cand[j])
        gt = cj > m2
        i2 = jnp.where(gt, j, i2)
        m2 = jnp.where(gt, cj, m2)
        s2 = jnp.where(gt, raw[j], s2)
    e1 = best * EXPERTS_PER_GROUP + i1
    e2 = best * EXPERTS_PER_GROUP + i2
    denom = s1 + s2
    g1 = s1 / denom
    g2 = s2 / denom

    eidx = lax.broadcasted_iota(jnp.int32, (N_EXPERTS, tm), 0)
    hit1 = eidx == e1
    hit2 = eidx == e2
    cnt = jnp.where(hit1, 1.0, jnp.where(hit2, 1.0, 0.0))
    tr = lax.broadcasted_iota(jnp.int32, (tm, tm), 0)
    tc = lax.broadcasted_iota(jnp.int32, (tm, tm), 1)
    before = jnp.where(tr < tc, 1.0, 0.0).astype(BF16)
    ranks = jnp.dot(cnt.astype(BF16), before, preferred_element_type=F32) + cnt_ref[:, 0:1]
    r1 = jnp.sum(jnp.where(hit1, ranks, 0.0), 0, keepdims=True)
    r2 = jnp.sum(jnp.where(hit2, ranks, 0.0), 0, keepdims=True)
    cnt_ref[...] = cnt_ref[...] + jnp.sum(cnt, 1, keepdims=True)
    meta_ref[...] = jnp.concatenate(
        [e1.astype(F32), e2.astype(F32), r1, r2, g1, g2, jnp.zeros((2, tm), F32)], 0)


def _mix(o_f, o_b, zuv, x2d, norm_w, sgu_g, sgu_b, w_sp, b_sp, w_out, ln_g, ln_b,
         w_router_t, router_bias, alpha, tm):
    n = x2d.shape[0]
    row = lambda i: (i, 0)
    const2 = lambda i: (0, 0)
    const3 = lambda i: (0, 0, 0)
    return pl.pallas_call(
        functools.partial(_mix_kernel, alpha=alpha),
        grid=(n // tm,),
        in_specs=[
            pl.BlockSpec((tm, D_DELTA), row),
            pl.BlockSpec((tm, D_DELTA), row),
            pl.BlockSpec((tm, D_DELTA), lambda i: (i, 0)),
            pl.BlockSpec((tm, D_SGU), lambda i: (i, 1)),
            pl.BlockSpec((tm, D_SGU), lambda i: (i, 2)),
            pl.BlockSpec((tm, D_MODEL), row),
            pl.BlockSpec((1, HEAD_DIM), const2),
            pl.BlockSpec((1, D_SGU), const2),
            pl.BlockSpec((1, D_SGU), const2),
            pl.BlockSpec((N_GROUPS_SGU, SGU_CHUNK, SGU_CHUNK), const3),
            pl.BlockSpec((N_GROUPS_SGU, SGU_CHUNK, HEAD_DIM), const3),
            pl.BlockSpec((D_MODEL, D_MODEL), const2),
            pl.BlockSpec((1, D_MODEL), const2),
            pl.BlockSpec((1, D_MODEL), const2),
            pl.BlockSpec((N_EXPERTS, D_MODEL), const2),
            pl.BlockSpec((N_EXPERTS, 128), const2),
        ],
        out_specs=[
            pl.BlockSpec((tm * SLAB, LANES), row),
            pl.BlockSpec((8, tm), lambda i: (0, i)),
            pl.BlockSpec((N_EXPERTS, 128), const2),
        ],
        out_shape=[
            jax.ShapeDtypeStruct((n * SLAB, LANES), F32),
            jax.ShapeDtypeStruct((8, n), F32),
            jax.ShapeDtypeStruct((N_EXPERTS, 128), F32),
        ],
        compiler_params=pltpu.CompilerParams(
            dimension_semantics=("arbitrary",), vmem_limit_bytes=VMEM_LIMIT),
        name="mix",
    )(o_f, o_b, zuv, zuv, zuv, x2d, norm_w, sgu_g, sgu_b, w_sp, b_sp, w_out, ln_g, ln_b,
      w_router_t, router_bias)


def _dispatch_kernel(p1_ref, p2_ref, zt_ref, x_ref, xs_hbm, zero_buf, sem, zsem):
    i = pl.program_id(0)
    td = x_ref.shape[0] // SLAB
    zrows = zero_buf.shape[0]

    @pl.when(i == 0)
    def _():
        zero_buf[...] = jnp.zeros_like(zero_buf)
        copies = [
            pltpu.make_async_copy(
                zero_buf,
                xs_hbm.at[pl.ds(pl.multiple_of(jnp.maximum(zt_ref[j], 0) * zrows, zrows), zrows), :],
                zsem)
            for j in range(zt_ref.shape[0])]
        for j, c in enumerate(copies):
            pl.when(zt_ref[j] >= 0)(c.start)
        for j, c in enumerate(copies):
            pl.when(zt_ref[j] >= 0)(c.wait)

    base = i * td
    for r in range(td):
        src = x_ref.at[pl.ds(r * SLAB, SLAB), :]
        for k, p_ref in enumerate((p1_ref, p2_ref)):
            row = pl.multiple_of(p_ref[base + r] * SLAB, SLAB)
            pltpu.make_async_copy(src, xs_hbm.at[pl.ds(row, SLAB), :], sem).start(priority=k)
    for _ in range(2):
        pltpu.make_async_copy(x_ref, xs_hbm.at[pl.ds(0, td * SLAB), :], sem).wait()


def _dispatch(pos1, pos2, zero_tiles, x1s, total_rows, tm_e, td):
    n = x1s.shape[0] // SLAB
    grid_spec = pltpu.PrefetchScalarGridSpec(
        num_scalar_prefetch=3,
        grid=(n // td,),
        in_specs=[pl.BlockSpec((td * SLAB, LANES), lambda i, p1, p2, zt: (i, 0))],
        out_specs=pl.BlockSpec(memory_space=pl.ANY),
        scratch_shapes=[pltpu.VMEM((tm_e * SLAB, LANES), F32), pltpu.SemaphoreType.DMA(()),
                        pltpu.SemaphoreType.DMA(())],
    )
    return pl.pallas_call(
        _dispatch_kernel,
        grid_spec=grid_spec,
        out_shape=jax.ShapeDtypeStruct((total_rows * SLAB, LANES), F32),
        compiler_params=pltpu.CompilerParams(
            dimension_semantics=("arbitrary",), vmem_limit_bytes=VMEM_LIMIT),
        name="dispatch",
    )(pos1, pos2, zero_tiles, x1s)


def _expert_kernel(te_ref, nt_ref, xs_ref, wg_ref, wu_ref, wd_ref, y_ref, wgb, wub, wdb):
    i = pl.program_id(0)
    tm = xs_ref.shape[0] // SLAB
    ntiles = nt_ref[0]

    @pl.when(i < ntiles)
    def _():
        @pl.when((i == 0) | (te_ref[i] != te_ref[jnp.maximum(i - 1, 0)]))
        def _():
            wgb[...] = wg_ref[...].astype(BF16)
            wub[...] = wu_ref[...].astype(BF16)
            wdb[...] = wd_ref[...].astype(BF16)

        xs = jnp.concatenate(_load_slabs(xs_ref, tm), -1).astype(BF16)
        hg = jnp.dot(xs, wgb[...], preferred_element_type=F32)
        hu = jnp.dot(xs, wub[...], preferred_element_type=F32)
        hid = (hg * jax.nn.sigmoid(hg) * hu).astype(BF16)
        _store_slabs(y_ref, jnp.dot(hid, wdb[...], preferred_element_type=F32))

    @pl.when(i >= ntiles)
    def _():
        y_ref[...] = jnp.zeros_like(y_ref)


def _experts(xs, tile_expert, ntiles, w_gate, w_up, w_down, layer, tm):
    max_tiles = tile_expert.shape[0]

    def wmap(i, te, nt):
        return (layer, te[jnp.minimum(i, nt[0] - 1)], 0, 0)

    wspec = pl.BlockSpec((None, None, D_MODEL, D_MODEL), wmap)
    grid_spec = pltpu.PrefetchScalarGridSpec(
        num_scalar_prefetch=2,
        grid=(max_tiles,),
        in_specs=[pl.BlockSpec((tm * SLAB, LANES), lambda i, te, nt: (jnp.minimum(i, nt[0] - 1), 0)),
                  wspec, wspec, wspec],
        out_specs=pl.BlockSpec((tm * SLAB, LANES), lambda i, te, nt: (i, 0)),
        scratch_shapes=[pltpu.VMEM((D_MODEL, D_MODEL), BF16)] * 3,
    )
    return pl.pallas_call(
        _expert_kernel,
        grid_spec=grid_spec,
        out_shape=jax.ShapeDtypeStruct(xs.shape, F32),
        compiler_params=pltpu.CompilerParams(
            dimension_semantics=("arbitrary",), vmem_limit_bytes=VMEM_LIMIT),
        name="experts",
    )(tile_expert, ntiles, xs, w_gate, w_up, w_down)


def _combine_kernel(p1_ref, p2_ref, y_hbm, x1_ref, gate_ref, lg_ref, lb_ref, o_ref, buf, sem,
                    *, alpha):
    i = pl.program_id(0)
    n_steps = pl.num_programs(0)
    tm = o_ref.shape[0]

    def gather(tile, slot):
        base = tile * tm
        for r in range(tm):
            for k, p_ref in enumerate((p1_ref, p2_ref)):
                row = pl.multiple_of(p_ref[base + r] * SLAB, SLAB)
                pltpu.make_async_copy(y_hbm.at[pl.ds(row, SLAB), :],
                                      buf.at[slot, k, pl.ds(r * SLAB, SLAB), :],
                                      sem.at[slot]).start(priority=k)

    def wait(slot):
        for k in range(2):
            pltpu.make_async_copy(y_hbm.at[pl.ds(0, tm * SLAB), :], buf.at[slot, k],
                                  sem.at[slot]).wait()

    @pl.when(i == 0)
    def _():
        gather(0, 0)

    def step(slot):
        wait(slot)
        gather(jnp.minimum(i + 1, n_steps - 1), 1 - slot)
        gates = gate_ref[...]
        g1, g2 = gates[:, 4:5], gates[:, 5:6]
        x1 = _load_slabs(x1_ref, tm)
        y1 = _load_slabs(buf.at[slot, 0], tm)
        y2 = _load_slabs(buf.at[slot, 1], tm)
        y = jnp.concatenate([alpha * x1[s] + g1 * y1[s] + g2 * y2[s] for s in range(SLAB)], -1)
        o_ref[...] = _layer_norm(y, lg_ref[...], lb_ref[...])

        @pl.when(i == n_steps - 1)
        def _():
            wait(1 - slot)

    for parity in range(2):
        pl.when(i % 2 == parity)(functools.partial(step, parity))


def _combine(pos1, pos2, ys, x1, gates, ln_g, ln_b, alpha, tm):
    n = x1.shape[0] // SLAB
    grid_spec = pltpu.PrefetchScalarGridSpec(
        num_scalar_prefetch=2,
        grid=(n // tm,),
        in_specs=[
            pl.BlockSpec(memory_space=pl.ANY),
            pl.BlockSpec((tm * SLAB, LANES), lambda i, p1, p2: (i, 0)),
            pl.BlockSpec((tm, 8), lambda i, p1, p2: (i, 0)),
            pl.BlockSpec((1, D_MODEL), lambda i, p1, p2: (0, 0)),
            pl.BlockSpec((1, D_MODEL), lambda i, p1, p2: (0, 0)),
        ],
        out_specs=pl.BlockSpec((tm, D_MODEL), lambda i, p1, p2: (i, 0)),
        scratch_shapes=[pltpu.VMEM((2, 2, tm * SLAB, LANES), F32), pltpu.SemaphoreType.DMA((2,))],
    )
    return pl.pallas_call(
        functools.partial(_combine_kernel, alpha=alpha),
        grid_spec=grid_spec,
        out_shape=jax.ShapeDtypeStruct((n, D_MODEL), F32),
        compiler_params=pltpu.CompilerParams(
            dimension_semantics=("arbitrary",), vmem_limit_bytes=VMEM_LIMIT),
        name="combine",
    )(pos1, pos2, ys, x1, gates, ln_g, ln_b)


def _routing_tables(meta, counts, tm_e):
    n = meta.shape[1]
    max_tiles = (2 * n) // tm_e + N_EXPERTS
    cnt = counts[:, 0].astype(jnp.int32)
    tiles = (cnt + tm_e - 1) // tm_e
    tile_end = jnp.cumsum(tiles)
    row_start = (tile_end - tiles) * tm_e
    e1 = meta[0].astype(jnp.int32)
    e2 = meta[1].astype(jnp.int32)
    pos1 = row_start[e1] + meta[2].astype(jnp.int32)
    pos2 = row_start[e2] + meta[3].astype(jnp.int32)
    ntiles = tile_end[-1:]
    tile_ids = jnp.arange(max_tiles, dtype=jnp.int32)
    tile_expert = jnp.minimum(
        jnp.sum((tile_ids[:, None] >= tile_end[None, :]).astype(jnp.int32), 1), N_EXPERTS - 1)
    tail = ntiles[0] + jnp.arange(N_EXPERTS, dtype=jnp.int32)
    zero_tiles = jnp.concatenate([
        jnp.where(tiles > 0, tile_end - 1, -1), jnp.where(tail < max_tiles, tail, -1)])
    return pos1, pos2, tile_expert, ntiles.astype(jnp.int32), zero_tiles.astype(jnp.int32)


def _forward(x, w_in, conv_w, a_log_f, dt_bias_f, a_log_b, dt_bias_b, delta_norm_w,
             sgu_ln_g, sgu_ln_b, w_spatial, b_spatial, w_out, ln1_g, ln1_b,
             w_router, router_bias, w_gate, w_up, w_down, ln2_g, ln2_b,
             *, tm_proj, bb_delta, sub_delta, tm_mix, td_disp, tm_exp, tm_comb):
    batch, seq, d = x.shape
    depth = w_in.shape[0]
    n = batch * seq
    alpha = (2.0 * depth) ** 0.25
    x2d = x.reshape(n, d)

    g0 = 4 * D_DELTA
    w_qkv = w_in[:, :, :3 * D_DELTA].astype(BF16)
    w_zuv = jnp.concatenate([w_in[:, :, 3 * D_DELTA:g0], w_in[:, :, g0 + 16:]], -1).astype(BF16)
    w_gl = jnp.pad(w_in[:, :, g0:g0 + 16], ((0, 0), (0, 0), (0, GATE_COLS - 16))).astype(BF16)
    conv_w8 = jnp.pad(conv_w, ((0, 0), (0, 8 - CONV_WIDTH), (0, 0)))
    pad_lanes = lambda a, b: jnp.pad(jnp.concatenate([a, b], -1), ((0, 0), (0, GATE_COLS - 8)))
    gate_params = jnp.pad(
        jnp.stack([pad_lanes(a_log_f, a_log_b), pad_lanes(dt_bias_f, dt_bias_b)], 1),
        ((0, 0), (0, 6), (0, 0)))
    w_sp = w_spatial.astype(BF16)
    b_sp = jnp.broadcast_to(b_spatial[..., None], b_spatial.shape + (HEAD_DIM,))
    w_out_b = w_out.astype(BF16)
    w_router_t = w_router.T
    rb = jnp.broadcast_to(router_bias[:, None], (N_EXPERTS, 128))
    assert (2 * n) % tm_exp == 0
    max_tiles = (2 * n) // tm_exp + N_EXPERTS

    for l in range(depth):
        qkv, zuv, gcol = _proj(x2d, w_qkv[l], w_zuv[l], w_gl[l], conv_w8[l], gate_params[l], seq,
                               tm_proj)
        gcol3 = gcol.reshape(batch, seq, 4 * N_HEADS)
        grow = gcol.reshape(batch, seq // CHUNK, CHUNK, 4 * N_HEADS).transpose(0, 1, 3, 2)
        o_f, o_b = _delta(qkv.reshape(batch, seq, 3 * D_DELTA), gcol3, grow, batch, seq, bb_delta,
                          sub_delta)
        o_f, o_b = o_f.reshape(n, D_DELTA), o_b.reshape(n, D_DELTA)
        x1, meta, counts = _mix(
            o_f, o_b, zuv, x2d, delta_norm_w[l][None], sgu_ln_g[l][None], sgu_ln_b[l][None],
            w_sp[l], b_sp[l], w_out_b[l], ln1_g[l][None], ln1_b[l][None], w_router_t, rb,
            alpha, tm_mix)
        pos1, pos2, tile_expert, ntiles, zero_tiles = _routing_tables(meta, counts, tm_exp)
        xs = _dispatch(pos1, pos2, zero_tiles, x1, max_tiles * tm_exp, tm_exp, td_disp)
        ys = _experts(xs, tile_expert, ntiles, w_gate, w_up, w_down, l, tm_exp)
        x2d = _combine(pos1, pos2, ys, x1, meta.T, ln2_g[l][None], ln2_b[l][None], alpha, tm_comb)
    return x2d.reshape(batch, seq, d)


def kernel(x, w_in, conv_w, a_log_f, dt_bias_f, a_log_b, dt_bias_b, delta_norm_w, sgu_ln_g,
           sgu_ln_b, w_spatial, b_spatial, w_out, ln1_g, ln1_b, w_router, router_bias, w_gate,
           w_up, w_down, ln2_g, ln2_b):
    return _forward(x, w_in, conv_w, a_log_f, dt_bias_f, a_log_b, dt_bias_b, delta_norm_w,
                    sgu_ln_g, sgu_ln_b, w_spatial, b_spatial, w_out, ln1_g, ln1_b, w_router,
                    router_bias, w_gate, w_up, w_down, ln2_g, ln2_b,
                    tm_proj=512, bb_delta=4, sub_delta=2, tm_mix=512, td_disp=1024, tm_exp=512,
                    tm_comb=256)
```

```python
import functools

import jax
import jax.numpy as jnp
from jax import lax
from jax.experimental import pallas as pl
from jax.experimental.pallas import tpu as pltpu

F32 = jnp.float32
BF16 = jnp.bfloat16

D_MODEL = 1024
D_DELTA = 512
D_SGU = 512
HEAD_DIM = 128
N_HEADS = 4
CONV_WIDTH = 5
CHUNK = 64
SGU_CHUNK = 128
N_GROUPS_SGU = 4
N_EXPERTS = 16
N_EXPERT_GROUPS = 4
EXPERTS_PER_GROUP = 4
LN_EPS = 1e-5
RMS_EPS = 1e-6
L2_EPS = 1e-6

GATE_COLS = 128
HALO_ROWS = 16
LANES = 128
SLAB = D_MODEL // LANES
VMEM_LIMIT = 56 * 1024 * 1024


def _bdot(a, b):
    return jnp.dot(a.astype(BF16), b.astype(BF16), preferred_element_type=F32)


def _bdot_nt(a, b):
    return lax.dot_general(a.astype(BF16), b.astype(BF16), (((1,), (1,)), ((), ())),
                           preferred_element_type=F32)


def _bdot_tn(a, b):
    return lax.dot_general(a.astype(BF16), b.astype(BF16), (((0,), (0,)), ((), ())),
                           preferred_element_type=F32)


def _layer_norm(y, g, b):
    mu = jnp.mean(y, -1, keepdims=True)
    d = y - mu
    var = jnp.mean(d * d, -1, keepdims=True)
    return d * lax.rsqrt(var + LN_EPS) * g + b


def _load_slabs(ref, rows):
    return [ref[pl.ds(s, rows, stride=SLAB), :] for s in range(SLAB)]


def _store_slabs(ref, mat):
    rows = mat.shape[0]
    for s in range(SLAB):
        ref[pl.ds(s, rows, stride=SLAB), :] = mat[:, s * LANES:(s + 1) * LANES]


def _conv_taps(ext, w, lo, n):
    rows = ext.shape[0]
    acc = None
    for j in range(CONV_WIDTH):
        shift = (-(j - CONV_WIDTH // 2)) % rows
        r = ext if shift == 0 else pltpu.roll(ext, shift, 0)
        term = r[lo:lo + n] * w[j:j + 1]
        acc = term if acc is None else acc + term
    return acc


def _conv_post(y, first_head):
    s = y * jax.nn.sigmoid(y)
    outs = []
    for j in range(y.shape[1] // HEAD_DIM):
        hh = first_head + j
        seg = s[:, j * HEAD_DIM:(j + 1) * HEAD_DIM]
        if hh < 2 * N_HEADS:
            ss = jnp.sum(seg * seg, -1, keepdims=True)
            seg = seg * lax.rsqrt(ss + L2_EPS)
            if hh < N_HEADS:
                seg = seg * (HEAD_DIM ** -0.5)
        outs.append(seg)
    return jnp.concatenate(outs, -1).astype(BF16)


def _proj_kernel(cur_ref, prev_ref, next_ref, wqkv_ref, wzuv_ref, wgl_ref, cw_ref, gp_ref,
                 qkv_ref, zuv_ref, gcol_ref, xe_ref, *, tiles_per_seq, col_chunk):
    i = pl.program_id(0)
    tb = cur_ref.shape[0]
    h = HALO_ROWS
    pos = i % tiles_per_seq
    xc = cur_ref[...].astype(BF16)
    xp = jnp.where(pos == 0, 0.0, prev_ref[...]).astype(BF16)
    xn = jnp.where(pos == tiles_per_seq - 1, 0.0, next_ref[...]).astype(BF16)
    xe_ref[0:h, :] = xp
    xe_ref[h:h + tb, :] = xc
    xe_ref[h + tb:, :] = xn
    n_chunks = 3 * D_DELTA // col_chunk
    cols = [slice(c * col_chunk, (c + 1) * col_chunk) for c in range(n_chunks)]

    def conv(c, h_ext):
        y = _conv_taps(h_ext, cw_ref[:, cols[c]], h, tb)
        qkv_ref[:, cols[c]] = _conv_post(y, c * col_chunk // HEAD_DIM)

    h_prev = jnp.dot(xe_ref[...], wqkv_ref[:, cols[0]], preferred_element_type=F32)
    for c in range(1, n_chunks):
        h_cur = jnp.dot(xe_ref[...], wqkv_ref[:, cols[c]], preferred_element_type=F32)
        conv(c - 1, h_prev)
        h_prev = h_cur
    for c in range(n_chunks):
        zuv_ref[:, cols[c]] = jnp.dot(xe_ref[h:h + tb, :], wzuv_ref[:, cols[c]],
                                      preferred_element_type=F32).astype(BF16)
        if c == 0:
            conv(n_chunks - 1, h_prev)

    gl = jnp.dot(xe_ref[h:h + tb, :], wgl_ref[...], preferred_element_type=F32)
    neg_a = -jnp.exp(gp_ref[0:1, :])
    xg = gl + gp_ref[1:2, :]
    softplus = jnp.maximum(xg, 0.0) + jnp.log1p(jnp.exp(-jnp.abs(xg)))
    g = neg_a * softplus
    beta = jax.nn.sigmoid(gl)
    lane = lax.broadcasted_iota(jnp.int32, (CHUNK, GATE_COLS), 1)
    r = lax.broadcasted_iota(jnp.int32, (2 * CHUNK, CHUNK), 0)
    c = lax.broadcasted_iota(jnp.int32, (2 * CHUNK, CHUNK), 1)
    tri = jnp.where(r < CHUNK, jnp.where(r >= c, 1.0, 0.0), jnp.where(r - CHUNK <= c, 1.0, 0.0))
    for ch in range(tb // CHUNK):
        sl = slice(ch * CHUNK, (ch + 1) * CHUNK)
        cs = jnp.dot(tri, g[sl], precision=lax.Precision.HIGHEST, preferred_element_type=F32)
        out = jnp.where(lane < N_HEADS, cs[:CHUNK],
                        jnp.where(lane < 2 * N_HEADS, cs[CHUNK:], beta[sl]))
        gcol_ref[sl, :] = out[:, :4 * N_HEADS]


def _proj(x2d, w_qkv, w_zuv, w_gl, conv_w8, gate_params, seq, tb):
    n = x2d.shape[0]
    hb = tb // HALO_ROWS
    last_hblk = n // HALO_ROWS - 1
    const = lambda i: (0, 0)
    return pl.pallas_call(
        functools.partial(_proj_kernel, tiles_per_seq=seq // tb, col_chunk=4 * HEAD_DIM),
        grid=(n // tb,),
        in_specs=[
            pl.BlockSpec((tb, D_MODEL), lambda i: (i, 0)),
            pl.BlockSpec((HALO_ROWS, D_MODEL), lambda i: (jnp.maximum(i * hb - 1, 0), 0)),
            pl.BlockSpec((HALO_ROWS, D_MODEL), lambda i: (jnp.minimum((i + 1) * hb, last_hblk), 0)),
            pl.BlockSpec((D_MODEL, 3 * D_DELTA), const),
            pl.BlockSpec((D_MODEL, 3 * D_DELTA), const),
            pl.BlockSpec((D_MODEL, GATE_COLS), const),
            pl.BlockSpec((8, 3 * D_DELTA), const),
            pl.BlockSpec((8, GATE_COLS), const),
        ],
        out_specs=[
            pl.BlockSpec((tb, 3 * D_DELTA), lambda i: (i, 0)),
            pl.BlockSpec((tb, 3 * D_DELTA), lambda i: (i, 0)),
            pl.BlockSpec((tb, 4 * N_HEADS), lambda i: (i, 0)),
        ],
        out_shape=[
            jax.ShapeDtypeStruct((n, 3 * D_DELTA), BF16),
            jax.ShapeDtypeStruct((n, 3 * D_DELTA), BF16),
            jax.ShapeDtypeStruct((n, 4 * N_HEADS), F32),
        ],
        scratch_shapes=[pltpu.VMEM((tb + 2 * HALO_ROWS, D_MODEL), BF16)],
        compiler_params=pltpu.CompilerParams(
            dimension_semantics=("parallel",), vmem_limit_bytes=VMEM_LIMIT),
        name="proj",
    )(x2d, x2d, x2d, w_qkv, w_zuv, w_gl, conv_w8, gate_params)


def _delta_chains(chains):
    c, nh = CHUNK, N_HEADS
    hc = nh * c
    n_ch = len(chains)
    every = range(n_ch)

    ri = lax.broadcasted_iota(jnp.int32, (hc, hc), 0)
    ci = lax.broadcasted_iota(jnp.int32, (hc, hc), 1)
    same64 = (ri >> 6) == (ci >> 6)
    same32 = (ri >> 5) == (ci >> 5)
    same16 = (ri >> 4) == (ci >> 4)
    eye = jnp.where(ri == ci, 1.0, 0.0)
    incl_of = {True: same64 & (ri >= ci), False: same64 & (ri <= ci)}
    strict_of = {d: m & (ri != ci) for d, m in incl_of.items()}

    q, k, v, gc_col, beta, g_last, k_dec, qg, decay, incl = ([] for _ in range(10))
    for qkv, gcol, grow, _, _, fwd in chains:
        off = 0 if fwd else nh

        def stack(base):
            return jnp.concatenate(
                [qkv[:, (base + h) * HEAD_DIM:(base + h + 1) * HEAD_DIM] for h in range(nh)], 0
            ).astype(F32)

        q.append(stack(0))
        k.append(stack(nh))
        v.append(stack(2 * nh))
        col = jnp.concatenate([gcol[:, off + h:off + h + 1] for h in range(nh)], 0)
        row = jnp.concatenate([grow[off + h:off + h + 1, :] for h in range(nh)], 1)
        gc_col.append(col)
        beta.append(jnp.concatenate(
            [gcol[:, 2 * nh + off + h:2 * nh + off + h + 1] for h in range(nh)], 0))
        edge = c - 1 if fwd else 0
        gl = [gcol[edge:edge + 1, off + h:off + h + 1] for h in range(nh)]
        g_last.append(gl)
        gl_col = jnp.concatenate([jnp.broadcast_to(g, (c, 1)) for g in gl], 0)
        k_dec.append(k[-1] * jnp.exp(gl_col - col))
        qg.append(q[-1] * jnp.exp(col))
        m = incl_of[fwd]
        incl.append(m)
        decay.append(jnp.where(m, jnp.exp(jnp.where(m, col - row, 0.0)), 0.0))

    kb = [k[i] * beta[i] for i in every]
    a = [jnp.where(strict_of[chains[i][5]], _bdot_nt(kb[i], k[i]) * decay[i], 0.0) for i in every]
    qk = [jnp.where(incl[i], _bdot_nt(q[i], k[i]) * decay[i], 0.0).astype(BF16) for i in every]

    p = [jnp.where(same16, a[i], 0.0) for i in every]
    t = [eye - p[i] for i in every]
    p = [x.astype(BF16) for x in p]
    tb = [x.astype(BF16) for x in t]
    for _ in range(3):
        p = [_bdot(p[i], p[i]).astype(BF16) for i in every]
        t = [t[i] + _bdot(tb[i], p[i]) for i in every]
        tb = [x.astype(BF16) for x in t]
    a32 = [jnp.where(same32, jnp.where(same16, 0.0, a[i]), 0.0).astype(BF16) for i in every]
    a64 = [jnp.where(same32, 0.0, a[i]).astype(BF16) for i in every]
    for a_off in (a32, a64):
        m = [_bdot(a_off[i], tb[i]).astype(BF16) for i in every]
        t = [t[i] - _bdot(tb[i], m[i]) for i in every]
        tb = [x.astype(BF16) for x in t]

    uw = [_bdot(tb[i], jnp.concatenate([v[i] * beta[i], kb[i] * jnp.exp(gc_col[i])], 1))
          for i in every]

    ws = [[_bdot(jnp.concatenate([uw[i][h * c:(h + 1) * c, HEAD_DIM:],
                                  qg[i][h * c:(h + 1) * c]], 0), chains[i][3][h])
           for h in range(nh)] for i in every]
    v_new = [[uw[i][h * c:(h + 1) * c, :HEAD_DIM] - ws[i][h][:c] for h in range(nh)]
             for i in every]
    o = [jnp.concatenate([ws[i][h][c:] for h in range(nh)], 0)
         + _bdot(qk[i], jnp.concatenate(v_new[i], 0)) for i in every]
    for i in every:
        s_ref, o_ref = chains[i][3], chains[i][4]
        for h in range(nh):
            rows = slice(h * c, (h + 1) * c)
            s_ref[h] = (s_ref[h] * jnp.exp(g_last[i][h])
                        + _bdot_tn(k_dec[i][rows], v_new[i][h]))
            o_ref[:, h * HEAD_DIM:(h + 1) * HEAD_DIM] = o[i][rows]


def _delta_kernel(qkv_f_ref, gcol_f_ref, grow_f_ref, qkv_b_ref, gcol_b_ref, grow_b_ref,
                  of_ref, ob_ref, sf_ref, sb_ref):
    @pl.when(pl.program_id(1) == 0)
    def _():
        sf_ref[...] = jnp.zeros_like(sf_ref)
        sb_ref[...] = jnp.zeros_like(sb_ref)

    n_sub = grow_f_ref.shape[1]
    for step in range(n_sub):
        cf, cb = step, n_sub - 1 - step
        rows_f, rows_b = pl.ds(cf * CHUNK, CHUNK), pl.ds(cb * CHUNK, CHUNK)
        chains = []
        for bb in range(qkv_f_ref.shape[0]):
            chains.append((qkv_f_ref[bb, rows_f, :], gcol_f_ref[bb, rows_f, :], grow_f_ref[bb, cf],
                           sf_ref.at[bb], of_ref.at[bb, rows_f, :], True))
            chains.append((qkv_b_ref[bb, rows_b, :], gcol_b_ref[bb, rows_b, :], grow_b_ref[bb, cb],
                           sb_ref.at[bb], ob_ref.at[bb, rows_b, :], False))
        _delta_chains(chains)


def _delta(qkv, gcol, grow, batch, seq, bb, n_sub):
    nc = seq // (CHUNK * n_sub)
    rows = CHUNK * n_sub
    fwd = lambda b, j: (b, j, 0)
    bwd = lambda b, j: (b, nc - 1 - j, 0)
    fwd4 = lambda b, j: (b, j, 0, 0)
    bwd4 = lambda b, j: (b, nc - 1 - j, 0, 0)
    return pl.pallas_call(
        _delta_kernel,
        grid=(batch // bb, nc),
        in_specs=[
            pl.BlockSpec((bb, rows, 3 * D_DELTA), fwd),
            pl.BlockSpec((bb, rows, 4 * N_HEADS), fwd),
            pl.BlockSpec((bb, n_sub, 4 * N_HEADS, CHUNK), fwd4),
            pl.BlockSpec((bb, rows, 3 * D_DELTA), bwd),
            pl.BlockSpec((bb, rows, 4 * N_HEADS), bwd),
            pl.BlockSpec((bb, n_sub, 4 * N_HEADS, CHUNK), bwd4),
        ],
        out_specs=[
            pl.BlockSpec((bb, rows, D_DELTA), fwd),
            pl.BlockSpec((bb, rows, D_DELTA), bwd),
        ],
        out_shape=[
            jax.ShapeDtypeStruct((batch, seq, D_DELTA), F32),
            jax.ShapeDtypeStruct((batch, seq, D_DELTA), F32),
        ],
        scratch_shapes=[
            pltpu.VMEM((bb, N_HEADS, HEAD_DIM, HEAD_DIM), F32),
            pltpu.VMEM((bb, N_HEADS, HEAD_DIM, HEAD_DIM), F32),
        ],
        compiler_params=pltpu.CompilerParams(
            dimension_semantics=("arbitrary", "arbitrary"), vmem_limit_bytes=VMEM_LIMIT),
        name="delta",
    )(qkv, gcol, grow, qkv, gcol, grow)


def _gelu(x):
    return 0.5 * x * (1.0 + lax.erf(x * (2.0 ** -0.5)))


def _split_bf16(x):
    hi = x.astype(BF16)
    lo = (x - hi.astype(F32)).astype(BF16)
    return hi, lo


def _mix_kernel(of_ref, ob_ref, z_ref, u_ref, v_ref, x_ref, nw_ref, sg_ref, sb_ref, wsp_ref,
                bsp_ref, wout_ref, lg_ref, lb_ref, wr_ref, rb_ref,
                x1_ref, meta_ref, cnt_ref, *, alpha):
    tm = x_ref.shape[0]

    @pl.when(pl.program_id(0) == 0)
    def _():
        cnt_ref[...] = jnp.zeros_like(cnt_ref)

    o = of_ref[...] + ob_ref[...]
    z = z_ref[...].astype(F32)
    nw = nw_ref[...]
    parts = []
    for h in range(N_HEADS):
        sl = slice(h * HEAD_DIM, (h + 1) * HEAD_DIM)
        seg = o[:, sl]
        seg = seg * lax.rsqrt(jnp.mean(seg * seg, -1, keepdims=True) + RMS_EPS) * nw
        zs = z[:, sl]
        parts.append((seg * (zs * jax.nn.sigmoid(zs))).astype(BF16))

    u = _gelu(u_ref[...].astype(F32))
    v = _layer_norm(_gelu(v_ref[...].astype(F32)), sg_ref[...], sb_ref[...]).astype(BF16)
    for g in range(N_GROUPS_SGU):
        sl = slice(g * HEAD_DIM, (g + 1) * HEAD_DIM)
        w_g = wsp_ref[g]
        blocks = []
        for ch in range(tm // SGU_CHUNK):
            rows = slice(ch * SGU_CHUNK, (ch + 1) * SGU_CHUNK)
            mixed = jnp.dot(w_g, v[rows, sl], preferred_element_type=F32) + bsp_ref[g]
            blocks.append(u[rows, sl] * mixed)
        parts.append(jnp.concatenate(blocks, 0).astype(BF16))

    mix = jnp.dot(jnp.concatenate(parts, -1), wout_ref[...], preferred_element_type=F32)
    x1 = _layer_norm(alpha * x_ref[...] + mix, lg_ref[...], lb_ref[...])
    _store_slabs(x1_ref, x1)

    x_hi, x_lo = _split_bf16(x1)
    w_hi, w_lo = _split_bf16(wr_ref[...])
    nt = (((1,), (1,)), ((), ()))
    logits = (lax.dot_general(w_hi, x_hi, nt, preferred_element_type=F32)
              + lax.dot_general(w_lo, x_hi, nt, preferred_element_type=F32)
              + lax.dot_general(w_hi, x_lo, nt, preferred_element_type=F32))
    scores = jax.nn.sigmoid(logits)
    sel = scores + rb_ref[:, 0:1]

    def row(m, e):
        return m[e:e + 1, :]

    best = jnp.zeros((1, tm), jnp.int32)
    best_score = None
    for gi in range(N_EXPERT_GROUPS):
        vals = [row(sel, gi * EXPERTS_PER_GROUP + j) for j in range(EXPERTS_PER_GROUP)]
        top2 = None
        for i in range(EXPERTS_PER_GROUP):
            for j in range(i + 1, EXPERTS_PER_GROUP):
                pair = vals[i] + vals[j]
                top2 = pair if top2 is None else jnp.maximum(top2, pair)
        if gi == 0:
            best_score = top2
        else:
            better = top2 > best_score
            best = jnp.where(better, gi, best)
            best_score = jnp.where(better, top2, best_score)

    def pick(m, j):
        out = row(m, j)
        for gi in range(1, N_EXPERT_GROUPS):
            out = jnp.where(best == gi, row(m, gi * EXPERTS_PER_GROUP + j), out)
        return out

    cand = [pick(sel, j) for j in range(EXPERTS_PER_GROUP)]
    raw = [pick(scores, j) for j in range(EXPERTS_PER_GROUP)]
    i1, m1, s1 = jnp.zeros((1, tm), jnp.int32), cand[0], raw[0]
    for j in range(1, EXPERTS_PER_GROUP):
        gt = cand[j] > m1
        i1 = jnp.where(gt, j, i1)
        m1 = jnp.where(gt, cand[j], m1)
        s1 = jnp.where(gt, raw[j], s1)
    i2 = jnp.zeros((1, tm), jnp.int32)
    m2 = jnp.full((1, tm), -jnp.inf, F32)
    s2 = jnp.zeros((1, tm), F32)
    for j in range(EXPERTS_PER_GROUP):
        cj = jnp.where(i1 == j, -jnp.inf, cand[j])
        gt = cj > m2
        i2 = jnp.where(gt, j, i2)
        m2 = jnp.where(gt, cj, m2)
        s2 = jnp.where(gt, raw[j], s2)
    e1 = best * EXPERTS_PER_GROUP + i1
    e2 = best * EXPERTS_PER_GROUP + i2
    denom = s1 + s2
    g1 = s1 / denom
    g2 = s2 / denom

    eidx = lax.broadcasted_iota(jnp.int32, (N_EXPERTS, tm), 0)
    hit1 = eidx == e1
    hit2 = eidx == e2
    cnt = jnp.where(hit1, 1.0, jnp.where(hit2, 1.0, 0.0))
    tr = lax.broadcasted_iota(jnp.int32, (tm, tm), 0)
    tc = lax.broadcasted_iota(jnp.int32, (tm, tm), 1)
    before = jnp.where(tr < tc, 1.0, 0.0).astype(BF16)
    ranks = jnp.dot(cnt.astype(BF16), before, preferred_element_type=F32) + cnt_ref[:, 0:1]
    r1 = jnp.sum(jnp.where(hit1, ranks, 0.0), 0, keepdims=True)
    r2 = jnp.sum(jnp.where(hit2, ranks, 0.0), 0, keepdims=True)
    cnt_ref[...] = cnt_ref[...] + jnp.sum(cnt, 1, keepdims=True)
    meta_ref[...] = jnp.concatenate(
        [e1.astype(F32), e2.astype(F32), r1, r2, g1, g2, jnp.zeros((2, tm), F32)], 0)


def _mix(o_f, o_b, zuv, x2d, norm_w, sgu_g, sgu_b, w_sp, b_sp, w_out, ln_g, ln_b,
         w_router_t, router_bias, alpha, tm):
    n = x2d.shape[0]
    row = lambda i: (i, 0)
    const2 = lambda i: (0, 0)
    const3 = lambda i: (0, 0, 0)
    return pl.pallas_call(
        functools.partial(_mix_kernel, alpha=alpha),
        grid=(n // tm,),
        in_specs=[
            pl.BlockSpec((tm, D_DELTA), row),
            pl.BlockSpec((tm, D_DELTA), row),
            pl.BlockSpec((tm, D_DELTA), lambda i: (i, 0)),
            pl.BlockSpec((tm, D_SGU), lambda i: (i, 1)),
            pl.BlockSpec((tm, D_SGU), lambda i: (i, 2)),
            pl.BlockSpec((tm, D_MODEL), row),
            pl.BlockSpec((1, HEAD_DIM), const2),
            pl.BlockSpec((1, D_SGU), const2),
            pl.BlockSpec((1, D_SGU), const2),
            pl.BlockSpec((N_GROUPS_SGU, SGU_CHUNK, SGU_CHUNK), const3),
            pl.BlockSpec((N_GROUPS_SGU, SGU_CHUNK, HEAD_DIM), const3),
            pl.BlockSpec((D_MODEL, D_MODEL), const2),
            pl.BlockSpec((1, D_MODEL), const2),
            pl.BlockSpec((1, D_MODEL), const2),
            pl.BlockSpec((N_EXPERTS, D_MODEL), const2),
            pl.BlockSpec((N_EXPERTS, 128), const2),
        ],
        out_specs=[
            pl.BlockSpec((tm * SLAB, LANES), row),
            pl.BlockSpec((8, tm), lambda i: (0, i)),
            pl.BlockSpec((N_EXPERTS, 128), const2),
        ],
        out_shape=[
            jax.ShapeDtypeStruct((n * SLAB, LANES), F32),
            jax.ShapeDtypeStruct((8, n), F32),
            jax.ShapeDtypeStruct((N_EXPERTS, 128), F32),
        ],
        compiler_params=pltpu.CompilerParams(
            dimension_semantics=("arbitrary",), vmem_limit_bytes=VMEM_LIMIT),
        name="mix",
    )(o_f, o_b, zuv, zuv, zuv, x2d, norm_w, sgu_g, sgu_b, w_sp, b_sp, w_out, ln_g, ln_b,
      w_router_t, router_bias)


def _dispatch_kernel(p1_ref, p2_ref, zt_ref, x_ref, xs_hbm, zero_buf, sem, zsem):
    i = pl.program_id(0)
    td = x_ref.shape[0] // SLAB
    zrows = zero_buf.shape[0]

    @pl.when(i == 0)
    def _():
        zero_buf[...] = jnp.zeros_like(zero_buf)
        copies = [
            pltpu.make_async_copy(
                zero_buf,
                xs_hbm.at[pl.ds(pl.multiple_of(jnp.maximum(zt_ref[j], 0) * zrows, zrows), zrows), :],
                zsem)
            for j in range(zt_ref.shape[0])]
        for j, c in enumerate(copies):
            pl.when(zt_ref[j] >= 0)(c.start)
        for j, c in enumerate(copies):
            pl.when(zt_ref[j] >= 0)(c.wait)

    base = i * td
    for r in range(td):
        src = x_ref.at[pl.ds(r * SLAB, SLAB), :]
        for k, p_ref in enumerate((p1_ref, p2_ref)):
            row = pl.multiple_of(p_ref[base + r] * SLAB, SLAB)
            pltpu.make_async_copy(src, xs_hbm.at[pl.ds(row, SLAB), :], sem).start(priority=k)
    for _ in range(2):
        pltpu.make_async_copy(x_ref, xs_hbm.at[pl.ds(0, td * SLAB), :], sem).wait()


def _dispatch(pos1, pos2, zero_tiles, x1s, total_rows, tm_e, td):
    n = x1s.shape[0] // SLAB
    grid_spec = pltpu.PrefetchScalarGridSpec(
        num_scalar_prefetch=3,
        grid=(n // td,),
        in_specs=[pl.BlockSpec((td * SLAB, LANES), lambda i, p1, p2, zt: (i, 0))],
        out_specs=pl.BlockSpec(memory_space=pl.ANY),
        scratch_shapes=[pltpu.VMEM((tm_e * SLAB, LANES), F32), pltpu.SemaphoreType.DMA(()),
                        pltpu.SemaphoreType.DMA(())],
    )
    return pl.pallas_call(
        _dispatch_kernel,
        grid_spec=grid_spec,
        out_shape=jax.ShapeDtypeStruct((total_rows * SLAB, LANES), F32),
        compiler_params=pltpu.CompilerParams(
            dimension_semantics=("arbitrary",), vmem_limit_bytes=VMEM_LIMIT),
        name="dispatch",
    )(pos1, pos2, zero_tiles, x1s)


def _expert_kernel(te_ref, nt_ref, xs_ref, wg_ref, wu_ref, wd_ref, y_ref, wgb, wub, wdb):
    i = pl.program_id(0)
    tm = xs_ref.shape[0] // SLAB
    ntiles = nt_ref[0]

    @pl.when(i < ntiles)
    def _():
        @pl.when((i == 0) | (te_ref[i] != te_ref[jnp.maximum(i - 1, 0)]))
        def _():
            wgb[...] = wg_ref[...].astype(BF16)
            wub[...] = wu_ref[...].astype(BF16)
            wdb[...] = wd_ref[...].astype(BF16)

        xs = jnp.concatenate(_load_slabs(xs_ref, tm), -1).astype(BF16)
        hg = jnp.dot(xs, wgb[...], preferred_element_type=F32)
        hu = jnp.dot(xs, wub[...], preferred_element_type=F32)
        hid = (hg * jax.nn.sigmoid(hg) * hu).astype(BF16)
        _store_slabs(y_ref, jnp.dot(hid, wdb[...], preferred_element_type=F32))

    @pl.when(i >= ntiles)
    def _():
        y_ref[...] = jnp.zeros_like(y_ref)


def _experts(xs, tile_expert, ntiles, w_gate, w_up, w_down, layer, tm):
    max_tiles = tile_expert.shape[0]

    def wmap(i, te, nt):
        return (layer, te[jnp.minimum(i, nt[0] - 1)], 0, 0)

    wspec = pl.BlockSpec((None, None, D_MODEL, D_MODEL), wmap)
    grid_spec = pltpu.PrefetchScalarGridSpec(
        num_scalar_prefetch=2,
        grid=(max_tiles,),
        in_specs=[pl.BlockSpec((tm * SLAB, LANES), lambda i, te, nt: (jnp.minimum(i, nt[0] - 1), 0)),
                  wspec, wspec, wspec],
        out_specs=pl.BlockSpec((tm * SLAB, LANES), lambda i, te, nt: (i, 0)),
        scratch_shapes=[pltpu.VMEM((D_MODEL, D_MODEL), BF16)] * 3,
    )
    return pl.pallas_call(
        _expert_kernel,
        grid_spec=grid_spec,
        out_shape=jax.ShapeDtypeStruct(xs.shape, F32),
        compiler_params=pltpu.CompilerParams(
            dimension_semantics=("arbitrary",), vmem_limit_bytes=VMEM_LIMIT),
        name="experts",
    )(tile_expert, ntiles, xs, w_gate, w_up, w_down)


def _combine_kernel(p1_ref, p2_ref, y_hbm, x1_ref, gate_ref, lg_ref, lb_ref, o_ref, buf, sem,
                    *, alpha):
    i = pl.program_id(0)
    n_steps = pl.num_programs(0)
    tm = o_ref.shape[0]

    def gather(tile, slot):
        base = tile * tm
        for r in range(tm):
            for k, p_ref in enumerate((p1_ref, p2_ref)):
                row = pl.multiple_of(p_ref[base + r] * SLAB, SLAB)
                pltpu.make_async_copy(y_hbm.at[pl.ds(row, SLAB), :],
                                      buf.at[slot, k, pl.ds(r * SLAB, SLAB), :],
                                      sem.at[slot]).start(priority=k)

    def wait(slot):
        for k in range(2):
            pltpu.make_async_copy(y_hbm.at[pl.ds(0, tm * SLAB), :], buf.at[slot, k],
                                  sem.at[slot]).wait()

    @pl.when(i == 0)
    def _():
        gather(0, 0)

    def step(slot):
        wait(slot)
        gather(jnp.minimum(i + 1, n_steps - 1), 1 - slot)
        gates = gate_ref[...]
        g1, g2 = gates[:, 4:5], gates[:, 5:6]
        x1 = _load_slabs(x1_ref, tm)
        y1 = _load_slabs(buf.at[slot, 0], tm)
        y2 = _load_slabs(buf.at[slot, 1], tm)
        y = jnp.concatenate([alpha * x1[s] + g1 * y1[s] + g2 * y2[s] for s in range(SLAB)], -1)
        o_ref[...] = _layer_norm(y, lg_ref[...], lb_ref[...])

        @pl.when(i == n_steps - 1)
        def _():
            wait(1 - slot)

    for parity in range(2):
        pl.when(i % 2 == parity)(functools.partial(step, parity))


def _combine(pos1, pos2, ys, x1, gates, ln_g, ln_b, alpha, tm):
    n = x1.shape[0] // SLAB
    grid_spec = pltpu.PrefetchScalarGridSpec(
        num_scalar_prefetch=2,
        grid=(n // tm,),
        in_specs=[
            pl.BlockSpec(memory_space=pl.ANY),
            pl.BlockSpec((tm * SLAB, LANES), lambda i, p1, p2: (i, 0)),
            pl.BlockSpec((tm, 8), lambda i, p1, p2: (i, 0)),
            pl.BlockSpec((1, D_MODEL), lambda i, p1, p2: (0, 0)),
            pl.BlockSpec((1, D_MODEL), lambda i, p1, p2: (0, 0)),
        ],
        out_specs=pl.BlockSpec((tm, D_MODEL), lambda i, p1, p2: (i, 0)),
        scratch_shapes=[pltpu.VMEM((2, 2, tm * SLAB, LANES), F32), pltpu.SemaphoreType.DMA((2,))],
    )
    return pl.pallas_call(
        functools.partial(_combine_kernel, alpha=alpha),
        grid_spec=grid_spec,
        out_shape=jax.ShapeDtypeStruct((n, D_MODEL), F32),
        compiler_params=pltpu.CompilerParams(
            dimension_semantics=("arbitrary",), vmem_limit_bytes=VMEM_LIMIT),
        name="combine",
    )(pos1, pos2, ys, x1, gates, ln_g, ln_b)


def _routing_tables(meta, counts, tm_e):
    n = meta.shape[1]
    max_tiles = (2 * n) // tm_e + N_EXPERTS
    cnt = counts[:, 0].astype(jnp.int32)
    tiles = (cnt + tm_e - 1) // tm_e
    tile_end = jnp.cumsum(tiles)
    row_start = (tile_end - tiles) * tm_e
    e1 = meta[0].astype(jnp.int32)
    e2 = meta[1].astype(jnp.int32)
    pos1 = row_start[e1] + meta[2].astype(jnp.int32)
    pos2 = row_start[e2] + meta[3].astype(jnp.int32)
    ntiles = tile_end[-1:]
    tile_ids = jnp.arange(max_tiles, dtype=jnp.int32)
    tile_expert = jnp.minimum(
        jnp.sum((tile_ids[:, None] >= tile_end[None, :]).astype(jnp.int32), 1), N_EXPERTS - 1)
    tail = ntiles[0] + jnp.arange(N_EXPERTS, dtype=jnp.int32)
    zero_tiles = jnp.concatenate([
        jnp.where(tiles > 0, tile_end - 1, -1), jnp.where(tail < max_tiles, tail, -1)])
    return pos1, pos2, tile_expert, ntiles.astype(jnp.int32), zero_tiles.astype(jnp.int32)


def _forward(x, w_in, conv_w, a_log_f, dt_bias_f, a_log_b, dt_bias_b, delta_norm_w,
             sgu_ln_g, sgu_ln_b, w_spatial, b_spatial, w_out, ln1_g, ln1_b,
             w_router, router_bias, w_gate, w_up, w_down, ln2_g, ln2_b,
             *, tm_proj, bb_delta, sub_delta, tm_mix, td_disp, tm_exp, tm_comb):
    batch, seq, d = x.shape
    depth = w_in.shape[0]
    n = batch * seq
    alpha = (2.0 * depth) ** 0.25
    x2d = x.reshape(n, d)

    g0 = 4 * D_DELTA
    w_qkv = w_in[:, :, :3 * D_DELTA].astype(BF16)
    w_zuv = jnp.concatenate([w_in[:, :, 3 * D_DELTA:g0], w_in[:, :, g0 + 16:]], -1).astype(BF16)
    w_gl = jnp.pad(w_in[:, :, g0:g0 + 16], ((0, 0), (0, 0), (0, GATE_COLS - 16))).astype(BF16)
    conv_w8 = jnp.pad(conv_w, ((0, 0), (0, 8 - CONV_WIDTH), (0, 0)))
    pad_lanes = lambda a, b: jnp.pad(jnp.concatenate([a, b], -1), ((0, 0), (0, GATE_COLS - 8)))
    gate_params = jnp.pad(
        jnp.stack([pad_lanes(a_log_f, a_log_b), pad_lanes(dt_bias_f, dt_bias_b)], 1),
        ((0, 0), (0, 6), (0, 0)))
    w_sp = w_spatial.astype(BF16)
    b_sp = jnp.broadcast_to(b_spatial[..., None], b_spatial.shape + (HEAD_DIM,))
    w_out_b = w_out.astype(BF16)
    w_router_t = w_router.T
    rb = jnp.broadcast_to(router_bias[:, None], (N_EXPERTS, 128))
    assert (2 * n) % tm_exp == 0
    max_tiles = (2 * n) // tm_exp + N_EXPERTS

    for l in range(depth):
        qkv, zuv, gcol = _proj(x2d, w_qkv[l], w_zuv[l], w_gl[l], conv_w8[l], gate_params[l], seq,
                               tm_proj)
        gcol3 = gcol.reshape(batch, seq, 4 * N_HEADS)
        grow = gcol.reshape(batch, seq // CHUNK, CHUNK, 4 * N_HEADS).transpose(0, 1, 3, 2)
        o_f, o_b = _delta(qkv.reshape(batch, seq, 3 * D_DELTA), gcol3, grow, batch, seq, bb_delta,
                          sub_delta)
        o_f, o_b = o_f.reshape(n, D_DELTA), o_b.reshape(n, D_DELTA)
        x1, meta, counts = _mix(
            o_f, o_b, zuv, x2d, delta_norm_w[l][None], sgu_ln_g[l][None], sgu_ln_b[l][None],
            w_sp[l], b_sp[l], w_out_b[l], ln1_g[l][None], ln1_b[l][None], w_router_t, rb,
            alpha, tm_mix)
        pos1, pos2, tile_expert, ntiles, zero_tiles = _routing_tables(meta, counts, tm_exp)
        xs = _dispatch(pos1, pos2, zero_tiles, x1, max_tiles * tm_exp, tm_exp, td_disp)
        ys = _experts(xs, tile_expert, ntiles, w_gate, w_up, w_down, l, tm_exp)
        x2d = _combine(pos1, pos2, ys, x1, meta.T, ln2_g[l][None], ln2_b[l][None], alpha, tm_comb)
    return x2d.reshape(batch, seq, d)


def kernel(x, w_in, conv_w, a_log_f, dt_bias_f, a_log_b, dt_bias_b, delta_norm_w, sgu_ln_g,
           sgu_ln_b, w_spatial, b_spatial, w_out, ln1_g, ln1_b, w_router, router_bias, w_gate,
           w_up, w_down, ln2_g, ln2_b):
    return _forward(x, w_in, conv_w, a_log_f, dt_bias_f, a_log_b, dt_bias_b, delta_norm_w,
                    sgu_ln_g, sgu_ln_b, w_spatial, b_spatial, w_out, ln1_g, ln1_b, w_router,
                    router_bias, w_gate, w_up, w_down, ln2_g, ln2_b,
                    tm_proj=1024, bb_delta=4, sub_delta=2, tm_mix=1024, td_disp=1024, tm_exp=512,
                    tm_comb=256)
```

```python
import functools

import jax
import jax.numpy as jnp
from jax import lax
from jax.experimental import pallas as pl
from jax.experimental.pallas import tpu as pltpu

F32 = jnp.float32
BF16 = jnp.bfloat16

D_MODEL = 1024
D_DELTA = 512
D_SGU = 512
HEAD_DIM = 128
N_HEADS = 4
CONV_WIDTH = 5
CHUNK = 64
SGU_CHUNK = 128
N_GROUPS_SGU = 4
N_EXPERTS = 16
N_EXPERT_GROUPS = 4
EXPERTS_PER_GROUP = 4
LN_EPS = 1e-5
RMS_EPS = 1e-6
L2_EPS = 1e-6

GATE_COLS = 128
HALO_ROWS = 16
LANES = 128
SLAB = D_MODEL // LANES
VMEM_LIMIT = 56 * 1024 * 1024


def _bdot(a, b):
    return jnp.dot(a.astype(BF16), b.astype(BF16), preferred_element_type=F32)


def _bdot_nt(a, b):
    return lax.dot_general(a.astype(BF16), b.astype(BF16), (((1,), (1,)), ((), ())),
                           preferred_element_type=F32)


def _bdot_tn(a, b):
    return lax.dot_general(a.astype(BF16), b.astype(BF16), (((0,), (0,)), ((), ())),
                           preferred_element_type=F32)


def _layer_norm(y, g, b):
    mu = jnp.mean(y, -1, keepdims=True)
    d = y - mu
    var = jnp.mean(d * d, -1, keepdims=True)
    return d * lax.rsqrt(var + LN_EPS) * g + b


def _load_slabs(ref, rows):
    return [ref[pl.ds(s, rows, stride=SLAB), :] for s in range(SLAB)]


def _store_slabs(ref, mat):
    rows = mat.shape[0]
    for s in range(SLAB):
        ref[pl.ds(s, rows, stride=SLAB), :] = mat[:, s * LANES:(s + 1) * LANES]


def _conv_taps(ext, w, lo, n):
    rows = ext.shape[0]
    acc = None
    for j in range(CONV_WIDTH):
        shift = (-(j - CONV_WIDTH // 2)) % rows
        r = ext if shift == 0 else pltpu.roll(ext, shift, 0)
        term = r[lo:lo + n] * w[j:j + 1]
        acc = term if acc is None else acc + term
    return acc


def _conv_post(y, first_head):
    s = y * jax.nn.sigmoid(y)
    outs = []
    for j in range(y.shape[1] // HEAD_DIM):
        hh = first_head + j
        seg = s[:, j * HEAD_DIM:(j + 1) * HEAD_DIM]
        if hh < 2 * N_HEADS:
            ss = jnp.sum(seg * seg, -1, keepdims=True)
            seg = seg * lax.rsqrt(ss + L2_EPS)
            if hh < N_HEADS:
                seg = seg * (HEAD_DIM ** -0.5)
        outs.append(seg)
    return jnp.concatenate(outs, -1).astype(BF16)


def _proj_kernel(cur_ref, prev_ref, next_ref, wqkv_ref, wzuv_ref, wgl_ref, cw_ref, gp_ref,
                 qkv_ref, zuv_ref, gcol_ref, xe_ref, *, tiles_per_seq, col_chunk):
    i = pl.program_id(0)
    tb = cur_ref.shape[0]
    h = HALO_ROWS
    pos = i % tiles_per_seq
    xc = cur_ref[...].astype(BF16)
    xp = jnp.where(pos == 0, 0.0, prev_ref[...]).astype(BF16)
    xn = jnp.where(pos == tiles_per_seq - 1, 0.0, next_ref[...]).astype(BF16)
    xe_ref[0:h, :] = xp
    xe_ref[h:h + tb, :] = xc
    xe_ref[h + tb:, :] = xn
    n_chunks = 3 * D_DELTA // col_chunk
    cols = [slice(c * col_chunk, (c + 1) * col_chunk) for c in range(n_chunks)]

    def conv(c, h_ext):
        y = _conv_taps(h_ext, cw_ref[:, cols[c]], h, tb)
        qkv_ref[:, cols[c]] = _conv_post(y, c * col_chunk // HEAD_DIM)

    h_prev = jnp.dot(xe_ref[...], wqkv_ref[:, cols[0]], preferred_element_type=F32)
    for c in range(1, n_chunks):
        h_cur = jnp.dot(xe_ref[...], wqkv_ref[:, cols[c]], preferred_element_type=F32)
        conv(c - 1, h_prev)
        h_prev = h_cur
    for c in range(n_chunks):
        zuv_ref[:, cols[c]] = jnp.dot(xe_ref[h:h + tb, :], wzuv_ref[:, cols[c]],
                                      preferred_element_type=F32).astype(BF16)
        if c == 0:
            conv(n_chunks - 1, h_prev)

    gl = jnp.dot(xe_ref[h:h + tb, :], wgl_ref[...], preferred_element_type=F32)
    neg_a = -jnp.exp(gp_ref[0:1, :])
    xg = gl + gp_ref[1:2, :]
    softplus = jnp.maximum(xg, 0.0) + jnp.log1p(jnp.exp(-jnp.abs(xg)))
    g = neg_a * softplus
    beta = jax.nn.sigmoid(gl)
    lane = lax.broadcasted_iota(jnp.int32, (CHUNK, GATE_COLS), 1)
    r = lax.broadcasted_iota(jnp.int32, (2 * CHUNK, CHUNK), 0)
    c = lax.broadcasted_iota(jnp.int32, (2 * CHUNK, CHUNK), 1)
    tri = jnp.where(r < CHUNK, jnp.where(r >= c, 1.0, 0.0), jnp.where(r - CHUNK <= c, 1.0, 0.0))
    for ch in range(tb // CHUNK):
        sl = slice(ch * CHUNK, (ch + 1) * CHUNK)
        cs = jnp.dot(tri, g[sl], precision=lax.Precision.HIGHEST, preferred_element_type=F32)
        out = jnp.where(lane < N_HEADS, cs[:CHUNK],
                        jnp.where(lane < 2 * N_HEADS, cs[CHUNK:], beta[sl]))
        gcol_ref[sl, :] = out[:, :4 * N_HEADS]


def _proj(x2d, w_qkv, w_zuv, w_gl, conv_w8, gate_params, seq, tb):
    n = x2d.shape[0]
    hb = tb // HALO_ROWS
    last_hblk = n // HALO_ROWS - 1
    const = lambda i: (0, 0)
    return pl.pallas_call(
        functools.partial(_proj_kernel, tiles_per_seq=seq // tb, col_chunk=4 * HEAD_DIM),
        grid=(n // tb,),
        in_specs=[
            pl.BlockSpec((tb, D_MODEL), lambda i: (i, 0)),
            pl.BlockSpec((HALO_ROWS, D_MODEL), lambda i: (jnp.maximum(i * hb - 1, 0), 0)),
            pl.BlockSpec((HALO_ROWS, D_MODEL), lambda i: (jnp.minimum((i + 1) * hb, last_hblk), 0)),
            pl.BlockSpec((D_MODEL, 3 * D_DELTA), const),
            pl.BlockSpec((D_MODEL, 3 * D_DELTA), const),
            pl.BlockSpec((D_MODEL, GATE_COLS), const),
            pl.BlockSpec((8, 3 * D_DELTA), const),
            pl.BlockSpec((8, GATE_COLS), const),
        ],
        out_specs=[
            pl.BlockSpec((tb, 3 * D_DELTA), lambda i: (i, 0)),
            pl.BlockSpec((tb, 3 * D_DELTA), lambda i: (i, 0)),
            pl.BlockSpec((tb, 4 * N_HEADS), lambda i: (i, 0)),
        ],
        out_shape=[
            jax.ShapeDtypeStruct((n, 3 * D_DELTA), BF16),
            jax.ShapeDtypeStruct((n, 3 * D_DELTA), BF16),
            jax.ShapeDtypeStruct((n, 4 * N_HEADS), F32),
        ],
        scratch_shapes=[pltpu.VMEM((tb + 2 * HALO_ROWS, D_MODEL), BF16)],
        compiler_params=pltpu.CompilerParams(
            dimension_semantics=("parallel",), vmem_limit_bytes=VMEM_LIMIT),
        name="proj",
    )(x2d, x2d, x2d, w_qkv, w_zuv, w_gl, conv_w8, gate_params)


def _delta_chains(chains):
    c, nh = CHUNK, N_HEADS
    hc = nh * c
    n_ch = len(chains)
    every = range(n_ch)

    ri = lax.broadcasted_iota(jnp.int32, (hc, hc), 0)
    ci = lax.broadcasted_iota(jnp.int32, (hc, hc), 1)
    same64 = (ri >> 6) == (ci >> 6)
    same32 = (ri >> 5) == (ci >> 5)
    same16 = (ri >> 4) == (ci >> 4)
    eye = jnp.where(ri == ci, 1.0, 0.0)
    incl_of = {True: same64 & (ri >= ci), False: same64 & (ri <= ci)}
    strict_of = {d: m & (ri != ci) for d, m in incl_of.items()}

    q, k, v, gc_col, beta, g_last, k_dec, qg, decay, incl = ([] for _ in range(10))
    for qkv, gcol, grow, _, _, fwd in chains:
        off = 0 if fwd else nh

        def stack(base):
            return jnp.concatenate(
                [qkv[:, (base + h) * HEAD_DIM:(base + h + 1) * HEAD_DIM] for h in range(nh)], 0
            ).astype(F32)

        q.append(stack(0))
        k.append(stack(nh))
        v.append(stack(2 * nh))
        col = jnp.concatenate([gcol[:, off + h:off + h + 1] for h in range(nh)], 0)
        row = jnp.concatenate([grow[off + h:off + h + 1, :] for h in range(nh)], 1)
        gc_col.append(col)
        beta.append(jnp.concatenate(
            [gcol[:, 2 * nh + off + h:2 * nh + off + h + 1] for h in range(nh)], 0))
        edge = c - 1 if fwd else 0
        gl = [gcol[edge:edge + 1, off + h:off + h + 1] for h in range(nh)]
        g_last.append(gl)
        gl_col = jnp.concatenate([jnp.broadcast_to(g, (c, 1)) for g in gl], 0)
        k_dec.append(k[-1] * jnp.exp(gl_col - col))
        qg.append(q[-1] * jnp.exp(col))
        m = incl_of[fwd]
        incl.append(m)
        decay.append(jnp.where(m, jnp.exp(jnp.where(m, col - row, 0.0)), 0.0))

    kb = [k[i] * beta[i] for i in every]
    a = [jnp.where(strict_of[chains[i][5]], _bdot_nt(kb[i], k[i]) * decay[i], 0.0) for i in every]
    qk = [jnp.where(incl[i], _bdot_nt(q[i], k[i]) * decay[i], 0.0).astype(BF16) for i in every]

    p = [jnp.where(same16, a[i], 0.0) for i in every]
    t = [eye - p[i] for i in every]
    p = [x.astype(BF16) for x in p]
    tb = [x.astype(BF16) for x in t]
    for _ in range(3):
        p = [_bdot(p[i], p[i]).astype(BF16) for i in every]
        t = [t[i] + _bdot(tb[i], p[i]) for i in every]
        tb = [x.astype(BF16) for x in t]
    a32 = [jnp.where(same32, jnp.where(same16, 0.0, a[i]), 0.0).astype(BF16) for i in every]
    a64 = [jnp.where(same32, 0.0, a[i]).astype(BF16) for i in every]
    for a_off in (a32, a64):
        m = [_bdot(a_off[i], tb[i]).astype(BF16) for i in every]
        t = [t[i] - _bdot(tb[i], m[i]) for i in every]
        tb = [x.astype(BF16) for x in t]

    uw = [_bdot(tb[i], jnp.concatenate([v[i] * beta[i], kb[i] * jnp.exp(gc_col[i])], 1))
          for i in every]

    ws = [[_bdot(jnp.concatenate([uw[i][h * c:(h + 1) * c, HEAD_DIM:],
                                  qg[i][h * c:(h + 1) * c]], 0), chains[i][3][h])
           for h in range(nh)] for i in every]
    v_new = [[uw[i][h * c:(h + 1) * c, :HEAD_DIM] - ws[i][h][:c] for h in range(nh)]
             for i in every]
    o = [jnp.concatenate([ws[i][h][c:] for h in range(nh)], 0)
         + _bdot(qk[i], jnp.concatenate(v_new[i], 0)) for i in every]
    for i in every:
        s_ref, o_ref = chains[i][3], chains[i][4]
        for h in range(nh):
            rows = slice(h * c, (h + 1) * c)
            s_ref[h] = (s_ref[h] * jnp.exp(g_last[i][h])
                        + _bdot_tn(k_dec[i][rows], v_new[i][h]))
            o_ref[:, h * HEAD_DIM:(h + 1) * HEAD_DIM] = o[i][rows]


def _delta_kernel(qkv_f_ref, gcol_f_ref, grow_f_ref, qkv_b_ref, gcol_b_ref, grow_b_ref,
                  of_ref, ob_ref, sf_ref, sb_ref):
    @pl.when(pl.program_id(1) == 0)
    def _():
        sf_ref[...] = jnp.zeros_like(sf_ref)
        sb_ref[...] = jnp.zeros_like(sb_ref)

    n_sub = grow_f_ref.shape[1]
    for step in range(n_sub):
        cf, cb = step, n_sub - 1 - step
        rows_f, rows_b = pl.ds(cf * CHUNK, CHUNK), pl.ds(cb * CHUNK, CHUNK)
        chains = []
        for bb in range(qkv_f_ref.shape[0]):
            chains.append((qkv_f_ref[bb, rows_f, :], gcol_f_ref[bb, rows_f, :], grow_f_ref[bb, cf],
                           sf_ref.at[bb], of_ref.at[bb, rows_f, :], True))
            chains.append((qkv_b_ref[bb, rows_b, :], gcol_b_ref[bb, rows_b, :], grow_b_ref[bb, cb],
                           sb_ref.at[bb], ob_ref.at[bb, rows_b, :], False))
        _delta_chains(chains)


def _delta(qkv, gcol, grow, batch, seq, bb, n_sub):
    nc = seq // (CHUNK * n_sub)
    rows = CHUNK * n_sub
    fwd = lambda b, j: (b, j, 0)
    bwd = lambda b, j: (b, nc - 1 - j, 0)
    fwd4 = lambda b, j: (b, j, 0, 0)
    bwd4 = lambda b, j: (b, nc - 1 - j, 0, 0)
    return pl.pallas_call(
        _delta_kernel,
        grid=(batch // bb, nc),
        in_specs=[
            pl.BlockSpec((bb, rows, 3 * D_DELTA), fwd),
            pl.BlockSpec((bb, rows, 4 * N_HEADS), fwd),
            pl.BlockSpec((bb, n_sub, 4 * N_HEADS, CHUNK), fwd4),
            pl.BlockSpec((bb, rows, 3 * D_DELTA), bwd),
            pl.BlockSpec((bb, rows, 4 * N_HEADS), bwd),
            pl.BlockSpec((bb, n_sub, 4 * N_HEADS, CHUNK), bwd4),
        ],
        out_specs=[
            pl.BlockSpec((bb, rows, D_DELTA), fwd),
            pl.BlockSpec((bb, rows, D_DELTA), bwd),
        ],
        out_shape=[
            jax.ShapeDtypeStruct((batch, seq, D_DELTA), F32),
            jax.ShapeDtypeStruct((batch, seq, D_DELTA), F32),
        ],
        scratch_shapes=[
            pltpu.VMEM((bb, N_HEADS, HEAD_DIM, HEAD_DIM), F32),
            pltpu.VMEM((bb, N_HEADS, HEAD_DIM, HEAD_DIM), F32),
        ],
        compiler_params=pltpu.CompilerParams(
            dimension_semantics=("arbitrary", "arbitrary"), vmem_limit_bytes=VMEM_LIMIT),
        name="delta",
    )(qkv, gcol, grow, qkv, gcol, grow)


def _gelu(x):
    return 0.5 * x * (1.0 + lax.erf(x * (2.0 ** -0.5)))


def _split_bf16(x):
    hi = x.astype(BF16)
    lo = (x - hi.astype(F32)).astype(BF16)
    return hi, lo


def _mix_kernel(of_ref, ob_ref, z_ref, u_ref, v_ref, x_ref, nw_ref, sg_ref, sb_ref, wsp_ref,
                bsp_ref, wout_ref, lg_ref, lb_ref, wr_ref, rb_ref,
                x1_ref, meta_ref, cnt_ref, *, alpha):
    tm = x_ref.shape[0]

    @pl.when(pl.program_id(0) == 0)
    def _():
        cnt_ref[...] = jnp.zeros_like(cnt_ref)

    o = of_ref[...] + ob_ref[...]
    z = z_ref[...].astype(F32)
    nw = nw_ref[...]
    parts = []
    for h in range(N_HEADS):
        sl = slice(h * HEAD_DIM, (h + 1) * HEAD_DIM)
        seg = o[:, sl]
        seg = seg * lax.rsqrt(jnp.mean(seg * seg, -1, keepdims=True) + RMS_EPS) * nw
        zs = z[:, sl]
        parts.append((seg * (zs * jax.nn.sigmoid(zs))).astype(BF16))

    u = _gelu(u_ref[...].astype(F32))
    v = _layer_norm(_gelu(v_ref[...].astype(F32)), sg_ref[...], sb_ref[...]).astype(BF16)
    for g in range(N_GROUPS_SGU):
        sl = slice(g * HEAD_DIM, (g + 1) * HEAD_DIM)
        w_g = wsp_ref[g]
        blocks = []
        for ch in range(tm // SGU_CHUNK):
            rows = slice(ch * SGU_CHUNK, (ch + 1) * SGU_CHUNK)
            mixed = jnp.dot(w_g, v[rows, sl], preferred_element_type=F32) + bsp_ref[g]
            blocks.append(u[rows, sl] * mixed)
        parts.append(jnp.concatenate(blocks, 0).astype(BF16))

    mix = jnp.dot(jnp.concatenate(parts, -1), wout_ref[...], preferred_element_type=F32)
    x1 = _layer_norm(alpha * x_ref[...] + mix, lg_ref[...], lb_ref[...])
    _store_slabs(x1_ref, x1)

    x_hi, x_lo = _split_bf16(x1)
    w_hi, w_lo = _split_bf16(wr_ref[...])
    nt = (((1,), (1,)), ((), ()))
    logits = (lax.dot_general(w_hi, x_hi, nt, preferred_element_type=F32)
              + lax.dot_general(w_lo, x_hi, nt, preferred_element_type=F32)
              + lax.dot_general(w_hi, x_lo, nt, preferred_element_type=F32))
    scores = jax.nn.sigmoid(logits)
    sel = scores + rb_ref[:, 0:1]

    def row(m, e):
        return m[e:e + 1, :]

    best = jnp.zeros((1, tm), jnp.int32)
    best_score = None
    for gi in range(N_EXPERT_GROUPS):
        vals = [row(sel, gi * EXPERTS_PER_GROUP + j) for j in range(EXPERTS_PER_GROUP)]
        top2 = None
        for i in range(EXPERTS_PER_GROUP):
            for j in range(i + 1, EXPERTS_PER_GROUP):
                pair = vals[i] + vals[j]
                top2 = pair if top2 is None else jnp.maximum(top2, pair)
        if gi == 0:
            best_score = top2
        else:
            better = top2 > best_score
            best = jnp.where(better, gi, best)
            best_score = jnp.where(better, top2, best_score)

    def pick(m, j):
        out = row(m, j)
        for gi in range(1, N_EXPERT_GROUPS):
            out = jnp.where(best == gi, row(m, gi * EXPERTS_PER_GROUP + j), out)
        return out

    cand = [pick(sel, j) for j in range(EXPERTS_PER_GROUP)]
    raw = [pick(scores, j) for j in range(EXPERTS_PER_GROUP)]
    i1, m1, s1 = jnp.zeros((1, tm), jnp.int32), cand[0], raw[0]
    for j in range(1, EXPERTS_PER_GROUP):
        gt = cand[j] > m1
        i1 = jnp.where(gt, j, i1)
        m1 = jnp.where(gt, cand[j], m1)
        s1 = jnp.where(gt, raw[j], s1)
    i2 = jnp.zeros((1, tm), jnp.int32)
    m2 = jnp.full((1, tm), -jnp.inf, F32)
    s2 = jnp.zeros((1, tm), F32)
    for j in range(EXPERTS_PER_GROUP):
        cj = jnp.where(i1 == j, -jnp.inf, cand[j])
        gt = cj > m2
        i2 = jnp.where(gt, j, i2)
        m2 = jnp.where(gt, cj, m2)
        s2 = jnp.where(gt, raw[j], s2)
    e1 = best * EXPERTS_PER_GROUP + i1
    e2 = best * EXPERTS_PER_GROUP + i2
    denom = s1 + s2
    g1 = s1 / denom
    g2 = s2 / denom

    eidx = lax.broadcasted_iota(jnp.int32, (N_EXPERTS, tm), 0)
    hit1 = eidx == e1
    hit2 = eidx == e2
    cnt = jnp.where(hit1, 1.0, jnp.where(hit2, 1.0, 0.0))
    tr = lax.broadcasted_iota(jnp.int32, (tm, tm), 0)
    tc = lax.broadcasted_iota(jnp.int32, (tm, tm), 1)
    before = jnp.where(tr < tc, 1.0, 0.0).astype(BF16)
    ranks = jnp.dot(cnt.astype(BF16), before, preferred_element_type=F32) + cnt_ref[:, 0:1]
    r1 = jnp.sum(jnp.where(hit1, ranks, 0.0), 0, keepdims=True)
    r2 = jnp.sum(jnp.where(hit2, ranks, 0.0), 0, keepdims=True)
    cnt_ref[...] = cnt_ref[...] + jnp.sum(cnt, 1, keepdims=True)
    meta_ref[...] = jnp.concatenate(
        [e1.astype(F32), e2.astype(F32), r1, r2, g1, g2, jnp.zeros((2, tm), F32)], 0)


def _mix(o_f, o_b, zuv, x2d, norm_w, sgu_g, sgu_b, w_sp, b_sp, w_out, ln_g, ln_b,
         w_router_t, router_bias, alpha, tm):
    n = x2d.shape[0]
    row = lambda i: (i, 0)
    const2 = lambda i: (0, 0)
    const3 = lambda i: (0, 0, 0)
    return pl.pallas_call(
        functools.partial(_mix_kernel, alpha=alpha),
        grid=(n // tm,),
        in_specs=[
            pl.BlockSpec((tm, D_DELTA), row),
            pl.BlockSpec((tm, D_DELTA), row),
            pl.BlockSpec((tm, D_DELTA), lambda i: (i, 0)),
            pl.BlockSpec((tm, D_SGU), lambda i: (i, 1)),
            pl.BlockSpec((tm, D_SGU), lambda i: (i, 2)),
            pl.BlockSpec((tm, D_MODEL), row),
            pl.BlockSpec((1, HEAD_DIM), const2),
            pl.BlockSpec((1, D_SGU), const2),
            pl.BlockSpec((1, D_SGU), const2),
            pl.BlockSpec((N_GROUPS_SGU, SGU_CHUNK, SGU_CHUNK), const3),
            pl.BlockSpec((N_GROUPS_SGU, SGU_CHUNK, HEAD_DIM), const3),
            pl.BlockSpec((D_MODEL, D_MODEL), const2),
            pl.BlockSpec((1, D_MODEL), const2),
            pl.BlockSpec((1, D_MODEL), const2),
            pl.BlockSpec((N_EXPERTS, D_MODEL), const2),
            pl.BlockSpec((N_EXPERTS, 128), const2),
        ],
        out_specs=[
            pl.BlockSpec((tm * SLAB, LANES), row),
            pl.BlockSpec((8, tm), lambda i: (0, i)),
            pl.BlockSpec((N_EXPERTS, 128), const2),
        ],
        out_shape=[
            jax.ShapeDtypeStruct((n * SLAB, LANES), F32),
            jax.ShapeDtypeStruct((8, n), F32),
            jax.ShapeDtypeStruct((N_EXPERTS, 128), F32),
        ],
        compiler_params=pltpu.CompilerParams(
            dimension_semantics=("arbitrary",), vmem_limit_bytes=VMEM_LIMIT),
        name="mix",
    )(o_f, o_b, zuv, zuv, zuv, x2d, norm_w, sgu_g, sgu_b, w_sp, b_sp, w_out, ln_g, ln_b,
      w_router_t, router_bias)


def _dispatch_kernel(p1_ref, p2_ref, zt_ref, x_ref, xs_hbm, zero_buf, sem, zsem):
    i = pl.program_id(0)
    td = x_ref.shape[0] // SLAB
    zrows = zero_buf.shape[0]

    @pl.when(i == 0)
    def _():
        zero_buf[...] = jnp.zeros_like(zero_buf)
        copies = [
            pltpu.make_async_copy(
                zero_buf,
                xs_hbm.at[pl.ds(pl.multiple_of(jnp.maximum(zt_ref[j], 0) * zrows, zrows), zrows), :],
                zsem)
            for j in range(zt_ref.shape[0])]
        for j, c in enumerate(copies):
            pl.when(zt_ref[j] >= 0)(c.start)
        for j, c in enumerate(copies):
            pl.when(zt_ref[j] >= 0)(c.wait)

    base = i * td
    for r in range(td):
        src = x_ref.at[pl.ds(r * SLAB, SLAB), :]
        for k, p_ref in enumerate((p1_ref, p2_ref)):
            row = pl.multiple_of(p_ref[base + r] * SLAB, SLAB)
            pltpu.make_async_copy(src, xs_hbm.at[pl.ds(row, SLAB), :], sem).start(priority=k)
    for _ in range(2):
        pltpu.make_async_copy(x_ref, xs_hbm.at[pl.ds(0, td * SLAB), :], sem).wait()


def _dispatch(pos1, pos2, zero_tiles, x1s, total_rows, tm_e, td):
    n = x1s.shape[0] // SLAB
    grid_spec = pltpu.PrefetchScalarGridSpec(
        num_scalar_prefetch=3,
        grid=(n // td,),
        in_specs=[pl.BlockSpec((td * SLAB, LANES), lambda i, p1, p2, zt: (i, 0))],
        out_specs=pl.BlockSpec(memory_space=pl.ANY),
        scratch_shapes=[pltpu.VMEM((tm_e * SLAB, LANES), F32), pltpu.SemaphoreType.DMA(()),
                        pltpu.SemaphoreType.DMA(())],
    )
    return pl.pallas_call(
        _dispatch_kernel,
        grid_spec=grid_spec,
        out_shape=jax.ShapeDtypeStruct((total_rows * SLAB, LANES), F32),
        compiler_params=pltpu.CompilerParams(
            dimension_semantics=("arbitrary",), vmem_limit_bytes=VMEM_LIMIT),
        name="dispatch",
    )(pos1, pos2, zero_tiles, x1s)


def _expert_kernel(te_ref, nt_ref, xs_ref, wg_ref, wu_ref, wd_ref, y_ref, wgb, wub, wdb):
    i = pl.program_id(0)
    tm = xs_ref.shape[0] // SLAB
    ntiles = nt_ref[0]

    @pl.when(i < ntiles)
    def _():
        @pl.when((i == 0) | (te_ref[i] != te_ref[jnp.maximum(i - 1, 0)]))
        def _():
            wgb[...] = wg_ref[...].astype(BF16)
            wub[...] = wu_ref[...].astype(BF16)
            wdb[...] = wd_ref[...].astype(BF16)

        xs = jnp.concatenate(_load_slabs(xs_ref, tm), -1).astype(BF16)
        hg = jnp.dot(xs, wgb[...], preferred_element_type=F32)
        hu = jnp.dot(xs, wub[...], preferred_element_type=F32)
        hid = (hg * jax.nn.sigmoid(hg) * hu).astype(BF16)
        _store_slabs(y_ref, jnp.dot(hid, wdb[...], preferred_element_type=F32))

    @pl.when(i >= ntiles)
    def _():
        y_ref[...] = jnp.zeros_like(y_ref)


def _experts(xs, tile_expert, ntiles, w_gate, w_up, w_down, layer, tm):
    max_tiles = tile_expert.shape[0]

    def wmap(i, te, nt):
        return (layer, te[jnp.minimum(i, nt[0] - 1)], 0, 0)

    wspec = pl.BlockSpec((None, None, D_MODEL, D_MODEL), wmap)
    grid_spec = pltpu.PrefetchScalarGridSpec(
        num_scalar_prefetch=2,
        grid=(max_tiles,),
        in_specs=[pl.BlockSpec((tm * SLAB, LANES), lambda i, te, nt: (jnp.minimum(i, nt[0] - 1), 0)),
                  wspec, wspec, wspec],
        out_specs=pl.BlockSpec((tm * SLAB, LANES), lambda i, te, nt: (i, 0)),
        scratch_shapes=[pltpu.VMEM((D_MODEL, D_MODEL), BF16)] * 3,
    )
    return pl.pallas_call(
        _expert_kernel,
        grid_spec=grid_spec,
        out_shape=jax.ShapeDtypeStruct(xs.shape, F32),
        compiler_params=pltpu.CompilerParams(
            dimension_semantics=("arbitrary",), vmem_limit_bytes=VMEM_LIMIT),
        name="experts",
    )(tile_expert, ntiles, xs, w_gate, w_up, w_down)


def _combine_kernel(p1_ref, p2_ref, y_hbm, x1_ref, gate_ref, lg_ref, lb_ref, o_ref, buf, sem,
                    *, alpha):
    i = pl.program_id(0)
    n_steps = pl.num_programs(0)
    tm = o_ref.shape[0]

    def gather(tile, slot):
        base = tile * tm
        for r in range(tm):
            for k, p_ref in enumerate((p1_ref, p2_ref)):
                row = pl.multiple_of(p_ref[base + r] * SLAB, SLAB)
                pltpu.make_async_copy(y_hbm.at[pl.ds(row, SLAB), :],
                                      buf.at[slot, k, pl.ds(r * SLAB, SLAB), :],
                                      sem.at[slot]).start(priority=k)

    def wait(slot):
        for k in range(2):
            pltpu.make_async_copy(y_hbm.at[pl.ds(0, tm * SLAB), :], buf.at[slot, k],
                                  sem.at[slot]).wait()

    @pl.when(i == 0)
    def _():
        gather(0, 0)

    def step(slot):
        wait(slot)
        gather(jnp.minimum(i + 1, n_steps - 1), 1 - slot)
        gates = gate_ref[...]
        g1, g2 = gates[:, 4:5], gates[:, 5:6]
        x1 = _load_slabs(x1_ref, tm)
        y1 = _load_slabs(buf.at[slot, 0], tm)
        y2 = _load_slabs(buf.at[slot, 1], tm)
        y = jnp.concatenate([alpha * x1[s] + g1 * y1[s] + g2 * y2[s] for s in range(SLAB)], -1)
        o_ref[...] = _layer_norm(y, lg_ref[...], lb_ref[...])

        @pl.when(i == n_steps - 1)
        def _():
            wait(1 - slot)

    for parity in range(2):
        pl.when(i % 2 == parity)(functools.partial(step, parity))


def _combine(pos1, pos2, ys, x1, gates, ln_g, ln_b, alpha, tm):
    n = x1.shape[0] // SLAB
    grid_spec = pltpu.PrefetchScalarGridSpec(
        num_scalar_prefetch=2,
        grid=(n // tm,),
        in_specs=[
            pl.BlockSpec(memory_space=pl.ANY),
            pl.BlockSpec((tm * SLAB, LANES), lambda i, p1, p2: (i, 0)),
            pl.BlockSpec((tm, 8), lambda i, p1, p2: (i, 0)),
            pl.BlockSpec((1, D_MODEL), lambda i, p1, p2: (0, 0)),
            pl.BlockSpec((1, D_MODEL), lambda i, p1, p2: (0, 0)),
        ],
        out_specs=pl.BlockSpec((tm, D_MODEL), lambda i, p1, p2: (i, 0)),
        scratch_shapes=[pltpu.VMEM((2, 2, tm * SLAB, LANES), F32), pltpu.SemaphoreType.DMA((2,))],
    )
    return pl.pallas_call(
        functools.partial(_combine_kernel, alpha=alpha),
        grid_spec=grid_spec,
        out_shape=jax.ShapeDtypeStruct((n, D_MODEL), F32),
        compiler_params=pltpu.CompilerParams(
            dimension_semantics=("arbitrary",), vmem_limit_bytes=VMEM_LIMIT),
        name="combine",
    )(pos1, pos2, ys, x1, gates, ln_g, ln_b)


def _routing_tables(meta, counts, tm_e):
    n = meta.shape[1]
    max_tiles = (2 * n) // tm_e + N_EXPERTS
    cnt = counts[:, 0].astype(jnp.int32)
    tiles = (cnt + tm_e - 1) // tm_e
    tile_end = jnp.cumsum(tiles)
    row_start = (tile_end - tiles) * tm_e
    e1 = meta[0].astype(jnp.int32)
    e2 = meta[1].astype(jnp.int32)
    pos1 = row_start[e1] + meta[2].astype(jnp.int32)
    pos2 = row_start[e2] + meta[3].astype(jnp.int32)
    ntiles = tile_end[-1:]
    tile_ids = jnp.arange(max_tiles, dtype=jnp.int32)
    tile_expert = jnp.minimum(
        jnp.sum((tile_ids[:, None] >= tile_end[None, :]).astype(jnp.int32), 1), N_EXPERTS - 1)
    tail = ntiles[0] + jnp.arange(N_EXPERTS, dtype=jnp.int32)
    zero_tiles = jnp.concatenate([
        jnp.where(tiles > 0, tile_end - 1, -1), jnp.where(tail < max_tiles, tail, -1)])
    return pos1, pos2, tile_expert, ntiles.astype(jnp.int32), zero_tiles.astype(jnp.int32)


def _forward(x, w_in, conv_w, a_log_f, dt_bias_f, a_log_b, dt_bias_b, delta_norm_w,
             sgu_ln_g, sgu_ln_b, w_spatial, b_spatial, w_out, ln1_g, ln1_b,
             w_router, router_bias, w_gate, w_up, w_down, ln2_g, ln2_b,
             *, tm_proj, bb_delta, sub_delta, tm_mix, td_disp, tm_exp, tm_comb):
    batch, seq, d = x.shape
    depth = w_in.shape[0]
    n = batch * seq
    alpha = (2.0 * depth) ** 0.25
    x2d = x.reshape(n, d)

    g0 = 4 * D_DELTA
    w_qkv = w_in[:, :, :3 * D_DELTA].astype(BF16)
    w_zuv = jnp.concatenate([w_in[:, :, 3 * D_DELTA:g0], w_in[:, :, g0 + 16:]], -1).astype(BF16)
    w_gl = jnp.pad(w_in[:, :, g0:g0 + 16], ((0, 0), (0, 0), (0, GATE_COLS - 16))).astype(BF16)
    conv_w8 = jnp.pad(conv_w, ((0, 0), (0, 8 - CONV_WIDTH), (0, 0)))
    pad_lanes = lambda a, b: jnp.pad(jnp.concatenate([a, b], -1), ((0, 0), (0, GATE_COLS - 8)))
    gate_params = jnp.pad(
        jnp.stack([pad_lanes(a_log_f, a_log_b), pad_lanes(dt_bias_f, dt_bias_b)], 1),
        ((0, 0), (0, 6), (0, 0)))
    w_sp = w_spatial.astype(BF16)
    b_sp = jnp.broadcast_to(b_spatial[..., None], b_spatial.shape + (HEAD_DIM,))
    w_out_b = w_out.astype(BF16)
    w_router_t = w_router.T
    rb = jnp.broadcast_to(router_bias[:, None], (N_EXPERTS, 128))
    assert (2 * n) % tm_exp == 0
    max_tiles = (2 * n) // tm_exp + N_EXPERTS

    for l in range(depth):
        qkv, zuv, gcol = _proj(x2d, w_qkv[l], w_zuv[l], w_gl[l], conv_w8[l], gate_params[l], seq,
                               tm_proj)
        gcol3 = gcol.reshape(batch, seq, 4 * N_HEADS)
        grow = gcol.reshape(batch, seq // CHUNK, CHUNK, 4 * N_HEADS).transpose(0, 1, 3, 2)
        o_f, o_b = _delta(qkv.reshape(batch, seq, 3 * D_DELTA), gcol3, grow, batch, seq, bb_delta,
                          sub_delta)
        o_f, o_b = o_f.reshape(n, D_DELTA), o_b.reshape(n, D_DELTA)
        x1, meta, counts = _mix(
            o_f, o_b, zuv, x2d, delta_norm_w[l][None], sgu_ln_g[l][None], sgu_ln_b[l][None],
            w_sp[l], b_sp[l], w_out_b[l], ln1_g[l][None], ln1_b[l][None], w_router_t, rb,
            alpha, tm_mix)
        pos1, pos2, tile_expert, ntiles, zero_tiles = _routing_tables(meta, counts, tm_exp)
        xs = _dispatch(pos1, pos2, zero_tiles, x1, max_tiles * tm_exp, tm_exp, td_disp)
        ys = _experts(xs, tile_expert, ntiles, w_gate, w_up, w_down, l, tm_exp)
        x2d = _combine(pos1, pos2, ys, x1, meta.T, ln2_g[l][None], ln2_b[l][None], alpha, tm_comb)
    return x2d.reshape(batch, seq, d)


def kernel(x, w_in, conv_w, a_log_f, dt_bias_f, a_log_b, dt_bias_b, delta_norm_w, sgu_ln_g,
           sgu_ln_b, w_spatial, b_spatial, w_out, ln1_g, ln1_b, w_router, router_bias, w_gate,
           w_up, w_down, ln2_g, ln2_b):
    return _forward(x, w_in, conv_w, a_log_f, dt_bias_f, a_log_b, dt_bias_b, delta_norm_w,
                    sgu_ln_g, sgu_ln_b, w_spatial, b_spatial, w_out, ln1_g, ln1_b, w_router,
                    router_bias, w_gate, w_up, w_down, ln2_g, ln2_b,
                    tm_proj=1024, bb_delta=4, sub_delta=4, tm_mix=1024, td_disp=1024, tm_exp=512,
                    tm_comb=256)
```

```python
import functools

import jax
import jax.numpy as jnp
from jax import lax
from jax.experimental import pallas as pl
from jax.experimental.pallas import tpu as pltpu

F32 = jnp.float32
BF16 = jnp.bfloat16

D_MODEL = 1024
D_DELTA = 512
D_SGU = 512
HEAD_DIM = 128
N_HEADS = 4
CONV_WIDTH = 5
CHUNK = 64
SGU_CHUNK = 128
N_GROUPS_SGU = 4
N_EXPERTS = 16
N_EXPERT_GROUPS = 4
EXPERTS_PER_GROUP = 4
LN_EPS = 1e-5
RMS_EPS = 1e-6
L2_EPS = 1e-6

GATE_COLS = 128
HALO_ROWS = 16
LANES = 128
SLAB = D_MODEL // LANES
VMEM_LIMIT = 56 * 1024 * 1024


def _bdot(a, b):
    return jnp.dot(a.astype(BF16), b.astype(BF16), preferred_element_type=F32)


def _bdot_nt(a, b):
    return lax.dot_general(a.astype(BF16), b.astype(BF16), (((1,), (1,)), ((), ())),
                           preferred_element_type=F32)


def _bdot_tn(a, b):
    return lax.dot_general(a.astype(BF16), b.astype(BF16), (((0,), (0,)), ((), ())),
                           preferred_element_type=F32)


def _silu(y):
    half = 0.5 * y
    return half + half * jnp.tanh(half)


def _layer_norm(y, g, b):
    mu = jnp.mean(y, -1, keepdims=True)
    d = y - mu
    var = jnp.mean(d * d, -1, keepdims=True)
    return d * lax.rsqrt(var + LN_EPS) * g + b


def _load_slabs(ref, rows):
    return [ref[pl.ds(s, rows, stride=SLAB), :] for s in range(SLAB)]


def _store_slabs(ref, mat):
    rows = mat.shape[0]
    for s in range(SLAB):
        ref[pl.ds(s, rows, stride=SLAB), :] = mat[:, s * LANES:(s + 1) * LANES]


def _conv_taps(ext, w, lo, n):
    rows = ext.shape[0]
    acc = None
    for j in range(CONV_WIDTH):
        shift = (-(j - CONV_WIDTH // 2)) % rows
        r = ext if shift == 0 else pltpu.roll(ext, shift, 0)
        term = r[lo:lo + n] * w[j:j + 1]
        acc = term if acc is None else acc + term
    return acc


def _conv_post(y, first_head):
    s = _silu(y)
    outs = []
    for j in range(y.shape[1] // HEAD_DIM):
        hh = first_head + j
        seg = s[:, j * HEAD_DIM:(j + 1) * HEAD_DIM]
        if hh < 2 * N_HEADS:
            ss = jnp.sum(seg * seg, -1, keepdims=True)
            seg = seg * lax.rsqrt(ss + L2_EPS)
            if hh < N_HEADS:
                seg = seg * (HEAD_DIM ** -0.5)
        outs.append(seg)
    return jnp.concatenate(outs, -1).astype(BF16)


def _proj_kernel(cur_ref, prev_ref, next_ref, wqkv_ref, wzuv_ref, wgl_ref, cw_ref, gp_ref,
                 qkv_ref, zuv_ref, gcol_ref, xe_ref, *, tiles_per_seq, col_chunk):
    i = pl.program_id(0)
    tb = cur_ref.shape[0]
    h = HALO_ROWS
    pos = i % tiles_per_seq
    xc = cur_ref[...].astype(BF16)
    xp = jnp.where(pos == 0, 0.0, prev_ref[...]).astype(BF16)
    xn = jnp.where(pos == tiles_per_seq - 1, 0.0, next_ref[...]).astype(BF16)
    xe_ref[0:h, :] = xp
    xe_ref[h:h + tb, :] = xc
    xe_ref[h + tb:, :] = xn
    n_chunks = 3 * D_DELTA // col_chunk
    cols = [slice(c * col_chunk, (c + 1) * col_chunk) for c in range(n_chunks)]

    def conv(c, h_ext):
        y = _conv_taps(h_ext, cw_ref[:, cols[c]], h, tb)
        qkv_ref[:, cols[c]] = _conv_post(y, c * col_chunk // HEAD_DIM)

    h_prev = jnp.dot(xe_ref[...], wqkv_ref[:, cols[0]], preferred_element_type=F32)
    for c in range(1, n_chunks):
        h_cur = jnp.dot(xe_ref[...], wqkv_ref[:, cols[c]], preferred_element_type=F32)
        conv(c - 1, h_prev)
        h_prev = h_cur
    for c in range(n_chunks):
        zuv_ref[:, cols[c]] = jnp.dot(xe_ref[h:h + tb, :], wzuv_ref[:, cols[c]],
                                      preferred_element_type=F32).astype(BF16)
        if c == 0:
            conv(n_chunks - 1, h_prev)

    gl = jnp.dot(xe_ref[h:h + tb, :], wgl_ref[...], preferred_element_type=F32)
    neg_a = -jnp.exp(gp_ref[0:1, :])
    xg = gl + gp_ref[1:2, :]
    softplus = jnp.maximum(xg, 0.0) + jnp.log1p(jnp.exp(-jnp.abs(xg)))
    g = neg_a * softplus
    beta = jax.nn.sigmoid(gl)
    lane = lax.broadcasted_iota(jnp.int32, (CHUNK, GATE_COLS), 1)
    r = lax.broadcasted_iota(jnp.int32, (2 * CHUNK, CHUNK), 0)
    c = lax.broadcasted_iota(jnp.int32, (2 * CHUNK, CHUNK), 1)
    tri = jnp.where(r < CHUNK, jnp.where(r >= c, 1.0, 0.0), jnp.where(r - CHUNK <= c, 1.0, 0.0))
    for ch in range(tb // CHUNK):
        sl = slice(ch * CHUNK, (ch + 1) * CHUNK)
        cs = jnp.dot(tri, g[sl], precision=lax.Precision.HIGHEST, preferred_element_type=F32)
        out = jnp.where(lane < N_HEADS, cs[:CHUNK],
                        jnp.where(lane < 2 * N_HEADS, cs[CHUNK:], beta[sl]))
        gcol_ref[sl, :] = out[:, :4 * N_HEADS]


def _proj(x2d, w_qkv, w_zuv, w_gl, conv_w8, gate_params, seq, tb):
    n = x2d.shape[0]
    hb = tb // HALO_ROWS
    last_hblk = n // HALO_ROWS - 1
    const = lambda i: (0, 0)
    return pl.pallas_call(
        functools.partial(_proj_kernel, tiles_per_seq=seq // tb, col_chunk=4 * HEAD_DIM),
        grid=(n // tb,),
        in_specs=[
            pl.BlockSpec((tb, D_MODEL), lambda i: (i, 0)),
            pl.BlockSpec((HALO_ROWS, D_MODEL), lambda i: (jnp.maximum(i * hb - 1, 0), 0)),
            pl.BlockSpec((HALO_ROWS, D_MODEL), lambda i: (jnp.minimum((i + 1) * hb, last_hblk), 0)),
            pl.BlockSpec((D_MODEL, 3 * D_DELTA), const),
            pl.BlockSpec((D_MODEL, 3 * D_DELTA), const),
            pl.BlockSpec((D_MODEL, GATE_COLS), const),
            pl.BlockSpec((8, 3 * D_DELTA), const),
            pl.BlockSpec((8, GATE_COLS), const),
        ],
        out_specs=[
            pl.BlockSpec((tb, 3 * D_DELTA), lambda i: (i, 0)),
            pl.BlockSpec((tb, 3 * D_DELTA), lambda i: (i, 0)),
            pl.BlockSpec((tb, 4 * N_HEADS), lambda i: (i, 0)),
        ],
        out_shape=[
            jax.ShapeDtypeStruct((n, 3 * D_DELTA), BF16),
            jax.ShapeDtypeStruct((n, 3 * D_DELTA), BF16),
            jax.ShapeDtypeStruct((n, 4 * N_HEADS), F32),
        ],
        scratch_shapes=[pltpu.VMEM((tb + 2 * HALO_ROWS, D_MODEL), BF16)],
        compiler_params=pltpu.CompilerParams(
            dimension_semantics=("parallel",), vmem_limit_bytes=VMEM_LIMIT),
        name="proj",
    )(x2d, x2d, x2d, w_qkv, w_zuv, w_gl, conv_w8, gate_params)


def _delta_chains(chains):
    c, nh = CHUNK, N_HEADS
    hc = nh * c
    n_ch = len(chains)
    every = range(n_ch)

    ri = lax.broadcasted_iota(jnp.int32, (hc, hc), 0)
    ci = lax.broadcasted_iota(jnp.int32, (hc, hc), 1)
    same64 = (ri >> 6) == (ci >> 6)
    same32 = (ri >> 5) == (ci >> 5)
    same16 = (ri >> 4) == (ci >> 4)
    eye = jnp.where(ri == ci, 1.0, 0.0)
    incl_of = {True: same64 & (ri >= ci), False: same64 & (ri <= ci)}
    strict_of = {d: m & (ri != ci) for d, m in incl_of.items()}

    q, k, v, gc_col, beta, g_last, k_dec, qg, decay, incl = ([] for _ in range(10))
    for qkv, gcol, grow, _, _, fwd in chains:
        off = 0 if fwd else nh

        def stack(base):
            return jnp.concatenate(
                [qkv[:, (base + h) * HEAD_DIM:(base + h + 1) * HEAD_DIM] for h in range(nh)], 0
            ).astype(F32)

        q.append(stack(0))
        k.append(stack(nh))
        v.append(stack(2 * nh))
        col = jnp.concatenate([gcol[:, off + h:off + h + 1] for h in range(nh)], 0)
        row = jnp.concatenate([grow[off + h:off + h + 1, :] for h in range(nh)], 1)
        gc_col.append(col)
        beta.append(jnp.concatenate(
            [gcol[:, 2 * nh + off + h:2 * nh + off + h + 1] for h in range(nh)], 0))
        edge = c - 1 if fwd else 0
        gl = [gcol[edge:edge + 1, off + h:off + h + 1] for h in range(nh)]
        g_last.append(gl)
        gl_col = jnp.concatenate([jnp.broadcast_to(g, (c, 1)) for g in gl], 0)
        k_dec.append(k[-1] * jnp.exp(gl_col - col))
        qg.append(q[-1] * jnp.exp(col))
        m = incl_of[fwd]
        incl.append(m)
        decay.append(jnp.where(m, jnp.exp(jnp.where(m, col - row, 0.0)), 0.0))

    kb = [k[i] * beta[i] for i in every]
    a = [jnp.where(strict_of[chains[i][5]], _bdot_nt(kb[i], k[i]) * decay[i], 0.0) for i in every]
    qk = [jnp.where(incl[i], _bdot_nt(q[i], k[i]) * decay[i], 0.0).astype(BF16) for i in every]

    p = [jnp.where(same16, a[i], 0.0) for i in every]
    t = [eye - p[i] for i in every]
    p = [x.astype(BF16) for x in p]
    tb = [x.astype(BF16) for x in t]
    for _ in range(3):
        p = [_bdot(p[i], p[i]).astype(BF16) for i in every]
        t = [t[i] + _bdot(tb[i], p[i]) for i in every]
        tb = [x.astype(BF16) for x in t]
    a32 = [jnp.where(same32, jnp.where(same16, 0.0, a[i]), 0.0).astype(BF16) for i in every]
    a64 = [jnp.where(same32, 0.0, a[i]).astype(BF16) for i in every]
    for a_off in (a32, a64):
        m = [_bdot(a_off[i], tb[i]).astype(BF16) for i in every]
        t = [t[i] - _bdot(tb[i], m[i]) for i in every]
        tb = [x.astype(BF16) for x in t]

    uw = [_bdot(tb[i], jnp.concatenate([v[i] * beta[i], kb[i] * jnp.exp(gc_col[i])], 1))
          for i in every]

    ws = [[_bdot(jnp.concatenate([uw[i][h * c:(h + 1) * c, HEAD_DIM:],
                                  qg[i][h * c:(h + 1) * c]], 0), chains[i][3][h])
           for h in range(nh)] for i in every]
    v_new = [[uw[i][h * c:(h + 1) * c, :HEAD_DIM] - ws[i][h][:c] for h in range(nh)]
             for i in every]
    o = [jnp.concatenate([ws[i][h][c:] for h in range(nh)], 0)
         + _bdot(qk[i], jnp.concatenate(v_new[i], 0)) for i in every]
    for i in every:
        s_ref, o_ref = chains[i][3], chains[i][4]
        for h in range(nh):
            rows = slice(h * c, (h + 1) * c)
            s_ref[h] = (s_ref[h] * jnp.exp(g_last[i][h])
                        + _bdot_tn(k_dec[i][rows], v_new[i][h]))
            o_ref[:, h * HEAD_DIM:(h + 1) * HEAD_DIM] = o[i][rows]


def _delta_kernel(qkv_f_ref, gcol_f_ref, grow_f_ref, qkv_b_ref, gcol_b_ref, grow_b_ref,
                  of_ref, ob_ref, sf_ref, sb_ref):
    @pl.when(pl.program_id(1) == 0)
    def _():
        sf_ref[...] = jnp.zeros_like(sf_ref)
        sb_ref[...] = jnp.zeros_like(sb_ref)

    n_sub = grow_f_ref.shape[1]
    for step in range(n_sub):
        cf, cb = step, n_sub - 1 - step
        rows_f, rows_b = pl.ds(cf * CHUNK, CHUNK), pl.ds(cb * CHUNK, CHUNK)
        chains = []
        for bb in range(qkv_f_ref.shape[0]):
            chains.append((qkv_f_ref[bb, rows_f, :], gcol_f_ref[bb, rows_f, :], grow_f_ref[bb, cf],
                           sf_ref.at[bb], of_ref.at[bb, rows_f, :], True))
            chains.append((qkv_b_ref[bb, rows_b, :], gcol_b_ref[bb, rows_b, :], grow_b_ref[bb, cb],
                           sb_ref.at[bb], ob_ref.at[bb, rows_b, :], False))
        _delta_chains(chains)


def _delta(qkv, gcol, grow, batch, seq, bb, n_sub):
    nc = seq // (CHUNK * n_sub)
    rows = CHUNK * n_sub
    fwd = lambda b, j: (b, j, 0)
    bwd = lambda b, j: (b, nc - 1 - j, 0)
    fwd4 = lambda b, j: (b, j, 0, 0)
    bwd4 = lambda b, j: (b, nc - 1 - j, 0, 0)
    return pl.pallas_call(
        _delta_kernel,
        grid=(batch // bb, nc),
        in_specs=[
            pl.BlockSpec((bb, rows, 3 * D_DELTA), fwd),
            pl.BlockSpec((bb, rows, 4 * N_HEADS), fwd),
            pl.BlockSpec((bb, n_sub, 4 * N_HEADS, CHUNK), fwd4),
            pl.BlockSpec((bb, rows, 3 * D_DELTA), bwd),
            pl.BlockSpec((bb, rows, 4 * N_HEADS), bwd),
            pl.BlockSpec((bb, n_sub, 4 * N_HEADS, CHUNK), bwd4),
        ],
        out_specs=[
            pl.BlockSpec((bb, rows, D_DELTA), fwd),
            pl.BlockSpec((bb, rows, D_DELTA), bwd),
        ],
        out_shape=[
            jax.ShapeDtypeStruct((batch, seq, D_DELTA), F32),
            jax.ShapeDtypeStruct((batch, seq, D_DELTA), F32),
        ],
        scratch_shapes=[
            pltpu.VMEM((bb, N_HEADS, HEAD_DIM, HEAD_DIM), F32),
            pltpu.VMEM((bb, N_HEADS, HEAD_DIM, HEAD_DIM), F32),
        ],
        compiler_params=pltpu.CompilerParams(
            dimension_semantics=("arbitrary", "arbitrary"), vmem_limit_bytes=VMEM_LIMIT),
        name="delta",
    )(qkv, gcol, grow, qkv, gcol, grow)


def _gelu(x):
    return 0.5 * x * (1.0 + lax.erf(x * (2.0 ** -0.5)))


def _split_bf16(x):
    hi = x.astype(BF16)
    lo = (x - hi.astype(F32)).astype(BF16)
    return hi, lo


def _mix_kernel(of_ref, ob_ref, z_ref, u_ref, v_ref, x_ref, nw_ref, sg_ref, sb_ref, wsp_ref,
                bsp_ref, wout_ref, lg_ref, lb_ref, wr_ref, rb_ref,
                x1_ref, meta_ref, cnt_ref, *, alpha):
    tm = x_ref.shape[0]

    @pl.when(pl.program_id(0) == 0)
    def _():
        cnt_ref[...] = jnp.zeros_like(cnt_ref)

    o = of_ref[...] + ob_ref[...]
    z = z_ref[...].astype(F32)
    nw = nw_ref[...]
    parts = []
    for h in range(N_HEADS):
        sl = slice(h * HEAD_DIM, (h + 1) * HEAD_DIM)
        seg = o[:, sl]
        seg = seg * lax.rsqrt(jnp.mean(seg * seg, -1, keepdims=True) + RMS_EPS) * nw
        zs = z[:, sl]
        parts.append((seg * _silu(zs)).astype(BF16))

    u = _gelu(u_ref[...].astype(F32))
    v = _layer_norm(_gelu(v_ref[...].astype(F32)), sg_ref[...], sb_ref[...]).astype(BF16)
    for g in range(N_GROUPS_SGU):
        sl = slice(g * HEAD_DIM, (g + 1) * HEAD_DIM)
        w_g = wsp_ref[g]
        blocks = []
        for ch in range(tm // SGU_CHUNK):
            rows = slice(ch * SGU_CHUNK, (ch + 1) * SGU_CHUNK)
            mixed = jnp.dot(w_g, v[rows, sl], preferred_element_type=F32) + bsp_ref[g]
            blocks.append(u[rows, sl] * mixed)
        parts.append(jnp.concatenate(blocks, 0).astype(BF16))

    mix = jnp.dot(jnp.concatenate(parts, -1), wout_ref[...], preferred_element_type=F32)
    x1 = _layer_norm(alpha * x_ref[...] + mix, lg_ref[...], lb_ref[...])
    _store_slabs(x1_ref, x1)

    x_hi, x_lo = _split_bf16(x1)
    w_hi, w_lo = _split_bf16(wr_ref[...])
    nt = (((1,), (1,)), ((), ()))
    logits = (lax.dot_general(w_hi, x_hi, nt, preferred_element_type=F32)
              + lax.dot_general(w_lo, x_hi, nt, preferred_element_type=F32)
              + lax.dot_general(w_hi, x_lo, nt, preferred_element_type=F32))
    scores = jax.nn.sigmoid(logits)
    sel = scores + rb_ref[:, 0:1]

    def row(m, e):
        return m[e:e + 1, :]

    best = jnp.zeros((1, tm), jnp.int32)
    best_score = None
    for gi in range(N_EXPERT_GROUPS):
        vals = [row(sel, gi * EXPERTS_PER_GROUP + j) for j in range(EXPERTS_PER_GROUP)]
        top2 = None
        for i in range(EXPERTS_PER_GROUP):
            for j in range(i + 1, EXPERTS_PER_GROUP):
                pair = vals[i] + vals[j]
                top2 = pair if top2 is None else jnp.maximum(top2, pair)
        if gi == 0:
            best_score = top2
        else:
            better = top2 > best_score
            best = jnp.where(better, gi, best)
            best_score = jnp.where(better, top2, best_score)

    def pick(m, j):
        out = row(m, j)
        for gi in range(1, N_EXPERT_GROUPS):
            out = jnp.where(best == gi, row(m, gi * EXPERTS_PER_GROUP + j), out)
        return out

    cand = [pick(sel, j) for j in range(EXPERTS_PER_GROUP)]
    raw = [pick(scores, j) for j in range(EXPERTS_PER_GROUP)]
    i1, m1, s1 = jnp.zeros((1, tm), jnp.int32), cand[0], raw[0]
    for j in range(1, EXPERTS_PER_GROUP):
        gt = cand[j] > m1
        i1 = jnp.where(gt, j, i1)
        m1 = jnp.where(gt, cand[j], m1)
        s1 = jnp.where(gt, raw[j], s1)
    i2 = jnp.zeros((1, tm), jnp.int32)
    m2 = jnp.full((1, tm), -jnp.inf, F32)
    s2 = jnp.zeros((1, tm), F32)
    for j in range(EXPERTS_PER_GROUP):
        cj = jnp.where(i1 == j, -jnp.inf, cand[j])
        gt = cj > m2
        i2 = jnp.where(gt, j, i2)
        m2 = jnp.where(gt, cj, m2)
        s2 = jnp.where(gt, raw[j], s2)
    e1 = best * EXPERTS_PER_GROUP + i1
    e2 = best * EXPERTS_PER_GROUP + i2
    denom = s1 + s2
    g1 = s1 / denom
    g2 = s2 / denom

    eidx = lax.broadcasted_iota(jnp.int32, (N_EXPERTS, tm), 0)
    hit1 = eidx == e1
    hit2 = eidx == e2
    cnt = jnp.where(hit1, 1.0, jnp.where(hit2, 1.0, 0.0))
    tr = lax.broadcasted_iota(jnp.int32, (tm, tm), 0)
    tc = lax.broadcasted_iota(jnp.int32, (tm, tm), 1)
    before = jnp.where(tr < tc, 1.0, 0.0).astype(BF16)
    ranks = jnp.dot(cnt.astype(BF16), before, preferred_element_type=F32) + cnt_ref[:, 0:1]
    r1 = jnp.sum(jnp.where(hit1, ranks, 0.0), 0, keepdims=True)
    r2 = jnp.sum(jnp.where(hit2, ranks, 0.0), 0, keepdims=True)
    cnt_ref[...] = cnt_ref[...] + jnp.sum(cnt, 1, keepdims=True)
    meta_ref[...] = jnp.concatenate(
        [e1.astype(F32), e2.astype(F32), r1, r2, g1, g2, jnp.zeros((2, tm), F32)], 0)


def _mix(o_f, o_b, zuv, x2d, norm_w, sgu_g, sgu_b, w_sp, b_sp, w_out, ln_g, ln_b,
         w_router_t, router_bias, alpha, tm):
    n = x2d.shape[0]
    row = lambda i: (i, 0)
    const2 = lambda i: (0, 0)
    const3 = lambda i: (0, 0, 0)
    return pl.pallas_call(
        functools.partial(_mix_kernel, alpha=alpha),
        grid=(n // tm,),
        in_specs=[
            pl.BlockSpec((tm, D_DELTA), row),
            pl.BlockSpec((tm, D_DELTA), row),
            pl.BlockSpec((tm, D_DELTA), lambda i: (i, 0)),
            pl.BlockSpec((tm, D_SGU), lambda i: (i, 1)),
            pl.BlockSpec((tm, D_SGU), lambda i: (i, 2)),
            pl.BlockSpec((tm, D_MODEL), row),
            pl.BlockSpec((1, HEAD_DIM), const2),
            pl.BlockSpec((1, D_SGU), const2),
            pl.BlockSpec((1, D_SGU), const2),
            pl.BlockSpec((N_GROUPS_SGU, SGU_CHUNK, SGU_CHUNK), const3),
            pl.BlockSpec((N_GROUPS_SGU, SGU_CHUNK, HEAD_DIM), const3),
            pl.BlockSpec((D_MODEL, D_MODEL), const2),
            pl.BlockSpec((1, D_MODEL), const2),
            pl.BlockSpec((1, D_MODEL), const2),
            pl.BlockSpec((N_EXPERTS, D_MODEL), const2),
            pl.BlockSpec((N_EXPERTS, 128), const2),
        ],
        out_specs=[
            pl.BlockSpec((tm * SLAB, LANES), row),
            pl.BlockSpec((8, tm), lambda i: (0, i)),
            pl.BlockSpec((N_EXPERTS, 128), const2),
        ],
        out_shape=[
            jax.ShapeDtypeStruct((n * SLAB, LANES), F32),
            jax.ShapeDtypeStruct((8, n), F32),
            jax.ShapeDtypeStruct((N_EXPERTS, 128), F32),
        ],
        compiler_params=pltpu.CompilerParams(
            dimension_semantics=("arbitrary",), vmem_limit_bytes=VMEM_LIMIT),
        name="mix",
    )(o_f, o_b, zuv, zuv, zuv, x2d, norm_w, sgu_g, sgu_b, w_sp, b_sp, w_out, ln_g, ln_b,
      w_router_t, router_bias)


def _dispatch_kernel(p1_ref, p2_ref, zt_ref, x_ref, xs_hbm, zero_buf, sem, zsem):
    i = pl.program_id(0)
    td = x_ref.shape[0] // SLAB
    zrows = zero_buf.shape[0]

    @pl.when(i == 0)
    def _():
        zero_buf[...] = jnp.zeros_like(zero_buf)
        copies = [
            pltpu.make_async_copy(
                zero_buf,
                xs_hbm.at[pl.ds(pl.multiple_of(jnp.maximum(zt_ref[j], 0) * zrows, zrows), zrows), :],
                zsem)
            for j in range(zt_ref.shape[0])]
        for j, c in enumerate(copies):
            pl.when(zt_ref[j] >= 0)(c.start)
        for j, c in enumerate(copies):
            pl.when(zt_ref[j] >= 0)(c.wait)

    base = i * td
    for r in range(td):
        src = x_ref.at[pl.ds(r * SLAB, SLAB), :]
        for k, p_ref in enumerate((p1_ref, p2_ref)):
            row = pl.multiple_of(p_ref[base + r] * SLAB, SLAB)
            pltpu.make_async_copy(src, xs_hbm.at[pl.ds(row, SLAB), :], sem).start(priority=k)
    for _ in range(2):
        pltpu.make_async_copy(x_ref, xs_hbm.at[pl.ds(0, td * SLAB), :], sem).wait()


def _dispatch(pos1, pos2, zero_tiles, x1s, total_rows, tm_e, td):
    n = x1s.shape[0] // SLAB
    grid_spec = pltpu.PrefetchScalarGridSpec(
        num_scalar_prefetch=3,
        grid=(n // td,),
        in_specs=[pl.BlockSpec((td * SLAB, LANES), lambda i, p1, p2, zt: (i, 0))],
        out_specs=pl.BlockSpec(memory_space=pl.ANY),
        scratch_shapes=[pltpu.VMEM((tm_e * SLAB, LANES), F32), pltpu.SemaphoreType.DMA(()),
                        pltpu.SemaphoreType.DMA(())],
    )
    return pl.pallas_call(
        _dispatch_kernel,
        grid_spec=grid_spec,
        out_shape=jax.ShapeDtypeStruct((total_rows * SLAB, LANES), F32),
        compiler_params=pltpu.CompilerParams(
            dimension_semantics=("arbitrary",), vmem_limit_bytes=VMEM_LIMIT),
        name="dispatch",
    )(pos1, pos2, zero_tiles, x1s)


def _expert_kernel(te_ref, nt_ref, xs_ref, wg_ref, wu_ref, wd_ref, y_ref, wgb, wub, wdb):
    i = pl.program_id(0)
    tm = xs_ref.shape[0] // SLAB
    ntiles = nt_ref[0]

    @pl.when(i < ntiles)
    def _():
        @pl.when((i == 0) | (te_ref[i] != te_ref[jnp.maximum(i - 1, 0)]))
        def _():
            wgb[...] = wg_ref[...].astype(BF16)
            wub[...] = wu_ref[...].astype(BF16)
            wdb[...] = wd_ref[...].astype(BF16)

        xs = jnp.concatenate(_load_slabs(xs_ref, tm), -1).astype(BF16)
        hg = jnp.dot(xs, wgb[...], preferred_element_type=F32)
        hu = jnp.dot(xs, wub[...], preferred_element_type=F32)
        hid = (_silu(hg) * hu).astype(BF16)
        _store_slabs(y_ref, jnp.dot(hid, wdb[...], preferred_element_type=F32))

    @pl.when(i >= ntiles)
    def _():
        y_ref[...] = jnp.zeros_like(y_ref)


def _experts(xs, tile_expert, ntiles, w_gate, w_up, w_down, layer, tm):
    max_tiles = tile_expert.shape[0]

    def wmap(i, te, nt):
        return (layer, te[jnp.minimum(i, nt[0] - 1)], 0, 0)

    wspec = pl.BlockSpec((None, None, D_MODEL, D_MODEL), wmap)
    grid_spec = pltpu.PrefetchScalarGridSpec(
        num_scalar_prefetch=2,
        grid=(max_tiles,),
        in_specs=[pl.BlockSpec((tm * SLAB, LANES), lambda i, te, nt: (jnp.minimum(i, nt[0] - 1), 0)),
                  wspec, wspec, wspec],
        out_specs=pl.BlockSpec((tm * SLAB, LANES), lambda i, te, nt: (i, 0)),
        scratch_shapes=[pltpu.VMEM((D_MODEL, D_MODEL), BF16)] * 3,
    )
    return pl.pallas_call(
        _expert_kernel,
        grid_spec=grid_spec,
        out_shape=jax.ShapeDtypeStruct(xs.shape, F32),
        compiler_params=pltpu.CompilerParams(
            dimension_semantics=("arbitrary",), vmem_limit_bytes=VMEM_LIMIT),
        name="experts",
    )(tile_expert, ntiles, xs, w_gate, w_up, w_down)


def _combine_kernel(p1_ref, p2_ref, y_hbm, x1_ref, gate_ref, lg_ref, lb_ref, o_ref, buf, sem,
                    *, alpha):
    i = pl.program_id(0)
    n_steps = pl.num_programs(0)
    tm = o_ref.shape[0]

    def gather(tile, slot):
        base = tile * tm
        for r in range(tm):
            for k, p_ref in enumerate((p1_ref, p2_ref)):
                row = pl.multiple_of(p_ref[base + r] * SLAB, SLAB)
                pltpu.make_async_copy(y_hbm.at[pl.ds(row, SLAB), :],
                                      buf.at[slot, k, pl.ds(r * SLAB, SLAB), :],
                                      sem.at[slot]).start(priority=k)

    def wait(slot):
        for k in range(2):
            pltpu.make_async_copy(y_hbm.at[pl.ds(0, tm * SLAB), :], buf.at[slot, k],
                                  sem.at[slot]).wait()

    @pl.when(i == 0)
    def _():
        gather(0, 0)

    def step(slot):
        wait(slot)
        gather(jnp.minimum(i + 1, n_steps - 1), 1 - slot)
        gates = gate_ref[...]
        g1, g2 = gates[:, 4:5], gates[:, 5:6]
        x1 = _load_slabs(x1_ref, tm)
        y1 = _load_slabs(buf.at[slot, 0], tm)
        y2 = _load_slabs(buf.at[slot, 1], tm)
        y = jnp.concatenate([alpha * x1[s] + g1 * y1[s] + g2 * y2[s] for s in range(SLAB)], -1)
        o_ref[...] = _layer_norm(y, lg_ref[...], lb_ref[...])

        @pl.when(i == n_steps - 1)
        def _():
            wait(1 - slot)

    for parity in range(2):
        pl.when(i % 2 == parity)(functools.partial(step, parity))


def _combine(pos1, pos2, ys, x1, gates, ln_g, ln_b, alpha, tm):
    n = x1.shape[0] // SLAB
    grid_spec = pltpu.PrefetchScalarGridSpec(
        num_scalar_prefetch=2,
        grid=(n // tm,),
        in_specs=[
            pl.BlockSpec(memory_space=pl.ANY),
            pl.BlockSpec((tm * SLAB, LANES), lambda i, p1, p2: (i, 0)),
            pl.BlockSpec((tm, 8), lambda i, p1, p2: (i, 0)),
            pl.BlockSpec((1, D_MODEL), lambda i, p1, p2: (0, 0)),
            pl.BlockSpec((1, D_MODEL), lambda i, p1, p2: (0, 0)),
        ],
        out_specs=pl.BlockSpec((tm, D_MODEL), lambda i, p1, p2: (i, 0)),
        scratch_shapes=[pltpu.VMEM((2, 2, tm * SLAB, LANES), F32), pltpu.SemaphoreType.DMA((2,))],
    )
    return pl.pallas_call(
        functools.partial(_combine_kernel, alpha=alpha),
        grid_spec=grid_spec,
        out_shape=jax.ShapeDtypeStruct((n, D_MODEL), F32),
        compiler_params=pltpu.CompilerParams(
            dimension_semantics=("arbitrary",), vmem_limit_bytes=VMEM_LIMIT),
        name="combine",
    )(pos1, pos2, ys, x1, gates, ln_g, ln_b)


def _routing_tables(meta, counts, tm_e):
    n = meta.shape[1]
    max_tiles = (2 * n) // tm_e + N_EXPERTS
    cnt = counts[:, 0].astype(jnp.int32)
    tiles = (cnt + tm_e - 1) // tm_e
    tile_end = jnp.cumsum(tiles)
    row_start = (tile_end - tiles) * tm_e
    e1 = meta[0].astype(jnp.int32)
    e2 = meta[1].astype(jnp.int32)
    pos1 = row_start[e1] + meta[2].astype(jnp.int32)
    pos2 = row_start[e2] + meta[3].astype(jnp.int32)
    ntiles = tile_end[-1:]
    tile_ids = jnp.arange(max_tiles, dtype=jnp.int32)
    tile_expert = jnp.minimum(
        jnp.sum((tile_ids[:, None] >= tile_end[None, :]).astype(jnp.int32), 1), N_EXPERTS - 1)
    tail = ntiles[0] + jnp.arange(N_EXPERTS, dtype=jnp.int32)
    zero_tiles = jnp.concatenate([
        jnp.where(tiles > 0, tile_end - 1, -1), jnp.where(tail < max_tiles, tail, -1)])
    return pos1, pos2, tile_expert, ntiles.astype(jnp.int32), zero_tiles.astype(jnp.int32)


def _forward(x, w_in, conv_w, a_log_f, dt_bias_f, a_log_b, dt_bias_b, delta_norm_w,
             sgu_ln_g, sgu_ln_b, w_spatial, b_spatial, w_out, ln1_g, ln1_b,
             w_router, router_bias, w_gate, w_up, w_down, ln2_g, ln2_b,
             *, tm_proj, bb_delta, sub_delta, tm_mix, td_disp, tm_exp, tm_comb):
    batch, seq, d = x.shape
    depth = w_in.shape[0]
    n = batch * seq
    alpha = (2.0 * depth) ** 0.25
    x2d = x.reshape(n, d)

    g0 = 4 * D_DELTA
    w_qkv = w_in[:, :, :3 * D_DELTA].astype(BF16)
    w_zuv = jnp.concatenate([w_in[:, :, 3 * D_DELTA:g0], w_in[:, :, g0 + 16:]], -1).astype(BF16)
    w_gl = jnp.pad(w_in[:, :, g0:g0 + 16], ((0, 0), (0, 0), (0, GATE_COLS - 16))).astype(BF16)
    conv_w8 = jnp.pad(conv_w, ((0, 0), (0, 8 - CONV_WIDTH), (0, 0)))
    pad_lanes = lambda a, b: jnp.pad(jnp.concatenate([a, b], -1), ((0, 0), (0, GATE_COLS - 8)))
    gate_params = jnp.pad(
        jnp.stack([pad_lanes(a_log_f, a_log_b), pad_lanes(dt_bias_f, dt_bias_b)], 1),
        ((0, 0), (0, 6), (0, 0)))
    w_sp = w_spatial.astype(BF16)
    b_sp = jnp.broadcast_to(b_spatial[..., None], b_spatial.shape + (HEAD_DIM,))
    w_out_b = w_out.astype(BF16)
    w_router_t = w_router.T
    rb = jnp.broadcast_to(router_bias[:, None], (N_EXPERTS, 128))
    assert (2 * n) % tm_exp == 0
    max_tiles = (2 * n) // tm_exp + N_EXPERTS

    for l in range(depth):
        qkv, zuv, gcol = _proj(x2d, w_qkv[l], w_zuv[l], w_gl[l], conv_w8[l], gate_params[l], seq,
                               tm_proj)
        gcol3 = gcol.reshape(batch, seq, 4 * N_HEADS)
        grow = gcol.reshape(batch, seq // CHUNK, CHUNK, 4 * N_HEADS).transpose(0, 1, 3, 2)
        o_f, o_b = _delta(qkv.reshape(batch, seq, 3 * D_DELTA), gcol3, grow, batch, seq, bb_delta,
                          sub_delta)
        o_f, o_b = o_f.reshape(n, D_DELTA), o_b.reshape(n, D_DELTA)
        x1, meta, counts = _mix(
            o_f, o_b, zuv, x2d, delta_norm_w[l][None], sgu_ln_g[l][None], sgu_ln_b[l][None],
            w_sp[l], b_sp[l], w_out_b[l], ln1_g[l][None], ln1_b[l][None], w_router_t, rb,
            alpha, tm_mix)
        pos1, pos2, tile_expert, ntiles, zero_tiles = _routing_tables(meta, counts, tm_exp)
        xs = _dispatch(pos1, pos2, zero_tiles, x1, max_tiles * tm_exp, tm_exp, td_disp)
        ys = _experts(xs, tile_expert, ntiles, w_gate, w_up, w_down, l, tm_exp)
        x2d = _combine(pos1, pos2, ys, x1, meta.T, ln2_g[l][None], ln2_b[l][None], alpha, tm_comb)
    return x2d.reshape(batch, seq, d)


def kernel(x, w_in, conv_w, a_log_f, dt_bias_f, a_log_b, dt_bias_b, delta_norm_w, sgu_ln_g,
           sgu_ln_b, w_spatial, b_spatial, w_out, ln1_g, ln1_b, w_router, router_bias, w_gate,
           w_up, w_down, ln2_g, ln2_b):
    return _forward(x, w_in, conv_w, a_log_f, dt_bias_f, a_log_b, dt_bias_b, delta_norm_w,
                    sgu_ln_g, sgu_ln_b, w_spatial, b_spatial, w_out, ln1_g, ln1_b, w_router,
                    router_bias, w_gate, w_up, w_down, ln2_g, ln2_b,
                    tm_proj=1024, bb_delta=4, sub_delta=4, tm_mix=1024, td_disp=1024, tm_exp=512,
                    tm_comb=256)
```
